```python
import jax, jax.numpy as jnp
from jax import lax
import numpy as np

D_MODEL = 1024
BATCH = 8
SEQ = 4096
DEPTH = 1

RWKV_HEADS = 8
RWKV_HEAD_DIM = 64
RWKV_DIM = RWKV_HEADS * RWKV_HEAD_DIM
DECAY_LORA = 64
AAA_LORA = 64
GATE_LORA = 128
RWKV_PROJ = 3 * RWKV_DIM + DECAY_LORA + AAA_LORA + GATE_LORA
LNX_EPS = 64e-5

SWA_Q_HEADS = 8
SWA_KV_HEADS = 2
SWA_GROUP = SWA_Q_HEADS // SWA_KV_HEADS
SWA_HEAD_DIM = 64
SWA_DIM = SWA_Q_HEADS * SWA_HEAD_DIM
SWA_KV_DIM = SWA_KV_HEADS * SWA_HEAD_DIM
SWA_PROJ = SWA_DIM + 2 * SWA_KV_DIM
WINDOW = 128
Q_BLOCK = 128
ALIBI_MAX = 8.0

MEM_LEN = 256
MEM_HEADS = 4
MEM_HEAD_DIM = 128
MEM_DIM = MEM_HEADS * MEM_HEAD_DIM

N_BRANCH = 3
BRANCH_DIM = 512
IN_PROJ = RWKV_PROJ + SWA_PROJ + MEM_DIM

N_EXPERTS = 32
TOP_K = 4
D_FF = 1024
SWIGLU_ALPHA = 1.702
SWIGLU_LIMIT = 7.0
EXPERT_BLOCK = 128

NORM_EPS = 1e-5

kernel_name = 'hybrid_rwkv7_swa_mem_moe_block'


def rms_norm(x, w):
    xf = x.astype(jnp.float32)
    y = xf * lax.rsqrt(jnp.mean(xf * xf, axis=-1, keepdims=True) + NORM_EPS)
    return (y * w).astype(x.dtype)


def token_shift_lerp(p, mu):
    prev = jnp.pad(p, ((0, 0), (1, 0), (0, 0)))[:, :-1]
    return p + (prev - p) * mu


def rwkv7_recurrence(r, decay, k, v, a, b):
    Bsz, T, H, N = r.shape

    def step(S, inp):
        r_t, w_t, k_t, v_t, a_t, b_t = inp
        Sa = jnp.einsum('bhvk,bhk->bhv', S, a_t)
        S = (S * w_t[:, :, None, :] + Sa[..., None] * b_t[:, :, None, :]
             + v_t[..., None] * k_t[:, :, None, :])
        return S, jnp.einsum('bhvk,bhk->bhv', S, r_t)

    xs = tuple(jnp.swapaxes(z, 0, 1) for z in (r, decay, k, v, a, b))
    S0 = jnp.zeros((Bsz, H, N, N), jnp.float32)
    _, y = lax.scan(step, S0, xs)
    return jnp.swapaxes(y, 0, 1)


def rwkv7_branch(p, mu, w0, w2, a0, a2, g2, k_k, k_a, r_k, lnx_w, lnx_b):
    Bsz, T, _ = p.shape
    f32 = jnp.float32
    p = token_shift_lerp(p, mu)
    r, k, v, pw, pa, pg = jnp.split(
        p, [RWKV_DIM, 2 * RWKV_DIM, 3 * RWKV_DIM, 3 * RWKV_DIM + DECAY_LORA,
            3 * RWKV_DIM + DECAY_LORA + AAA_LORA], axis=-1)
    w_log = -jax.nn.softplus(-(w0 + jnp.tanh(pw) @ w2).astype(f32)) - 0.5
    decay = jnp.exp(-jnp.exp(w_log))
    a = jax.nn.sigmoid((a0 + pa @ a2).astype(f32))
    g = jax.nn.sigmoid(pg) @ g2

    def heads(z):
        return z.astype(f32).reshape(Bsz, T, RWKV_HEADS, RWKV_HEAD_DIM)

    kk = heads(k * k_k)
    kk = kk / jnp.maximum(jnp.linalg.norm(kk, axis=-1, keepdims=True), 1e-12)
    k = k.astype(f32) * (1.0 + (a - 1.0) * k_a)
    rh, kh, vh, ah = heads(r), heads(k), heads(v), heads(a)
    y = rwkv7_recurrence(rh, heads(decay), kh, vh, -kk, kk * ah)
    m = jnp.mean(y, axis=-1, keepdims=True)
    var = jnp.mean(jnp.square(y - m), axis=-1, keepdims=True)
    y = ((y - m) * lax.rsqrt(var + LNX_EPS)).reshape(Bsz, T, RWKV_DIM) * lnx_w + lnx_b
    bonus = jnp.sum(rh * kh * r_k, axis=-1, keepdims=True) * vh
    y = y + bonus.reshape(Bsz, T, RWKV_DIM)
    return (y * g).astype(p.dtype)


def alibi_slopes(n_heads):
    return 2.0 ** (-ALIBI_MAX * jnp.arange(1, n_heads + 1, dtype=jnp.float32) / n_heads)


def swa_sink_alibi_branch(p, sinks):
    Bsz, T, _ = p.shape
    nb = T // Q_BLOCK
    f32 = jnp.float32
    q, k, v = jnp.split(p, [SWA_DIM, SWA_DIM + SWA_KV_DIM], axis=-1)
    q = q.reshape(Bsz, nb, Q_BLOCK, SWA_KV_HEADS, SWA_GROUP, SWA_HEAD_DIM)

    def band(z):
        z = z.reshape(Bsz, T, SWA_KV_HEADS, SWA_HEAD_DIM)
        prev = jnp.pad(z, ((0, 0), (Q_BLOCK, 0), (0, 0), (0, 0)))[:, :T]
        shp = (Bsz, nb, Q_BLOCK, SWA_KV_HEADS, SWA_HEAD_DIM)
        return jnp.concatenate([prev.reshape(shp), z.reshape(shp)], axis=2)

    kb, vb = band(k), band(v)
    scores = jnp.einsum('bnqhgd,bnkhd->bnhgqk', q, kb).astype(f32) * (SWA_HEAD_DIM ** -0.5)
    i = jnp.arange(Q_BLOCK)[:, None]
    j = jnp.arange(2 * Q_BLOCK)[None, :]
    dist = i + Q_BLOCK - j
    s_pos = jnp.arange(nb)[:, None, None] * Q_BLOCK - Q_BLOCK + j[None]
    valid = (dist >= 0) & (dist < WINDOW) & (s_pos >= 0)
    slopes = alibi_slopes(SWA_Q_HEADS).reshape(SWA_KV_HEADS, SWA_GROUP)
    scores = scores - slopes[:, :, None, None] * dist.astype(f32)
    scores = jnp.where(valid[None, :, None, None], scores, -jnp.inf)
    sink = jnp.broadcast_to(sinks.astype(f32).reshape(SWA_KV_HEADS, SWA_GROUP, 1, 1),
                            scores.shape[:-1] + (1,))
    probs = jax.nn.softmax(jnp.concatenate([scores, sink], axis=-1), axis=-1)[..., :-1]
    out = jnp.einsum('bnhgqk,bnkhd->bnqhgd', probs.astype(p.dtype), vb)
    return out.reshape(Bsz, T, SWA_DIM)


def memory_cross_branch(q, mem_n, w_mem_kv):
    Bsz, T, _ = q.shape
    M = mem_n.shape[1]
    km, vm = jnp.split(mem_n @ w_mem_kv, 2, axis=-1)
    qh = q.reshape(Bsz, T, MEM_HEADS, MEM_HEAD_DIM)
    km = km.reshape(Bsz, M, MEM_HEADS, MEM_HEAD_DIM)
    vm = vm.reshape(Bsz, M, MEM_HEADS, MEM_HEAD_DIM)
    scores = jnp.einsum('bthd,bmhd->bhtm', qh, km).astype(jnp.float32) * (MEM_HEAD_DIM ** -0.5)
    probs = jax.nn.softmax(scores, axis=-1).astype(q.dtype)
    return jnp.einsum('bhtm,bmhd->bthd', probs, vm).reshape(Bsz, T, MEM_DIM)


def hybrid_mixer(xn, mem_n, w_in, b_swa, rwkv_mu, rwkv_w0, rwkv_w2, rwkv_a0, rwkv_a2,
                 rwkv_g2, rwkv_kk, rwkv_ka, rwkv_rk, rwkv_lnx_w, rwkv_lnx_b, swa_sinks,
                 w_mem_kv, w_gate, w_branch, w_out):
    p = xn @ w_in
    p_rwkv, p_swa, p_mem = jnp.split(p, [RWKV_PROJ, RWKV_PROJ + SWA_PROJ], axis=-1)
    y_rwkv = rwkv7_branch(p_rwkv, rwkv_mu, rwkv_w0, rwkv_w2, rwkv_a0, rwkv_a2, rwkv_g2,
                          rwkv_kk, rwkv_ka, rwkv_rk, rwkv_lnx_w, rwkv_lnx_b)
    y_swa = swa_sink_alibi_branch(p_swa + b_swa, swa_sinks)
    y_mem = memory_cross_branch(p_mem, mem_n, w_mem_kv)
    branches = (y_rwkv, y_swa, y_mem)
    merged = jax.nn.sigmoid(xn @ w_gate[0]) * (branches[0] @ w_branch[0])
    for bi in range(1, N_BRANCH):
        merged = merged + jax.nn.sigmoid(xn @ w_gate[bi]) * (branches[bi] @ w_branch[bi])
    return merged @ w_out


def clamped_swiglu(h):
    h_glu, h_lin = h[..., ::2], h[..., 1::2]
    h_glu = jnp.minimum(h_glu, SWIGLU_LIMIT)
    h_lin = jnp.clip(h_lin, -SWIGLU_LIMIT, SWIGLU_LIMIT)
    return h_glu * jax.nn.sigmoid(SWIGLU_ALPHA * h_glu) * (h_lin + 1.0)


def sparse_moe(xn, w_router, b_router, w1, b1, w2, b2):
    Bsz, T, D = xn.shape
    xt = xn.reshape(-1, D)
    N = xt.shape[0]
    logits = (xt @ w_router + b_router).astype(jnp.float32)
    top_vals, top_idx = lax.top_k(logits, TOP_K)
    gates = jax.nn.softmax(top_vals, axis=-1)
    A = N * TOP_K
    flat_e = top_idx.reshape(-1).astype(jnp.int32)
    flat_tok = jnp.arange(A, dtype=jnp.int32) // TOP_K
    flat_g = gates.reshape(-1).astype(xt.dtype)
    order = jnp.argsort(flat_e)
    e_sorted = flat_e[order]
    counts = jnp.bincount(flat_e, length=N_EXPERTS)
    starts = jnp.cumsum(counts) - counts
    padded = ((counts + EXPERT_BLOCK - 1) // EXPERT_BLOCK) * EXPERT_BLOCK
    pad_ends = jnp.cumsum(padded)
    pad_starts = pad_ends - padded
    dest = pad_starts[e_sorted] + (jnp.arange(A, dtype=jnp.int32) - starts[e_sorted])
    n_blocks = -(-A // EXPERT_BLOCK) + N_EXPERTS
    P = n_blocks * EXPERT_BLOCK
    slot_tok = jnp.full((P,), N, jnp.int32).at[dest].set(flat_tok[order])
    slot_g = jnp.zeros((P,), xt.dtype).at[dest].set(flat_g[order])
    block_e = jnp.minimum(
        jnp.searchsorted(pad_ends, jnp.arange(n_blocks) * EXPERT_BLOCK, side='right'),
        N_EXPERTS - 1)
    x_pad = jnp.concatenate([xt, jnp.zeros((1, D), xt.dtype)], axis=0)

    def run_block(args):
        tok, e = args
        h = x_pad[tok] @ w1[e] + b1[e]
        return clamped_swiglu(h) @ w2[e] + b2[e]

    y = lax.map(run_block, (slot_tok.reshape(n_blocks, EXPERT_BLOCK), block_e))
    y = y.reshape(P, D) * slot_g[:, None]
    out = jnp.zeros((N + 1, D), y.dtype).at[slot_tok].add(y)[:N]
    return out.reshape(Bsz, T, D)


def setup_inputs(seed: int = 0) -> dict:
    key = jax.random.key(seed)
    ks = jax.random.split(key, 30)
    f32 = jnp.float32
    L, D = DEPTH, D_MODEL

    def nrm(i, shape, scale):
        return jax.random.normal(ks[i], shape, f32) * scale

    def unif(i, shape, lo, hi):
        return jax.random.uniform(ks[i], shape, f32, lo, hi)

    return {
        'x': nrm(0, (BATCH, SEQ, D), 1.0),
        'mem': nrm(1, (BATCH, MEM_LEN, D), 1.0),
        'attn_norm_w': 1.0 + nrm(2, (L, D), 0.02),
        'mem_norm_w': 1.0 + nrm(3, (L, D), 0.02),
        'w_in': nrm(4, (L, D, IN_PROJ), D ** -0.5),
        'b_swa': nrm(5, (L, SWA_PROJ), 0.01),
        'rwkv_mu': unif(6, (L, RWKV_PROJ), 0.0, 1.0),
        'rwkv_w0': unif(7, (L, RWKV_DIM), -6.0, 1.0),
        'rwkv_w2': nrm(8, (L, DECAY_LORA, RWKV_DIM), 0.1 * DECAY_LORA ** -0.5),
        'rwkv_a0': nrm(9, (L, RWKV_DIM), 0.1),
        'rwkv_a2': nrm(10, (L, AAA_LORA, RWKV_DIM), 0.1 * AAA_LORA ** -0.5),
        'rwkv_g2': nrm(11, (L, GATE_LORA, RWKV_DIM), GATE_LORA ** -0.5),
        'rwkv_kk': 0.85 + nrm(12, (L, RWKV_DIM), 0.02),
        'rwkv_ka': 1.0 + nrm(13, (L, RWKV_DIM), 0.02),
        'rwkv_rk': nrm(14, (L, RWKV_HEADS, RWKV_HEAD_DIM), 0.1),
        'rwkv_lnx_w': 1.0 + nrm(15, (L, RWKV_DIM), 0.02),
        'rwkv_lnx_b': nrm(16, (L, RWKV_DIM), 0.01),
        'swa_sinks': nrm(17, (L, SWA_Q_HEADS), 1.0),
        'w_mem_kv': nrm(18, (L, D, 2 * MEM_DIM), D ** -0.5),
        'w_gate': nrm(19, (L, N_BRANCH, D, D), D ** -0.5),
        'w_branch': nrm(20, (L, N_BRANCH, BRANCH_DIM, D), BRANCH_DIM ** -0.5),
        'w_out': nrm(21, (L, D, D), D ** -0.5),
        'ffn_norm_w': 1.0 + nrm(22, (L, D), 0.02),
        'w_router': nrm(23, (L, D, N_EXPERTS), D ** -0.5),
        'b_router': nrm(24, (L, N_EXPERTS), 0.01),
        'w_exp1': nrm(25, (L, N_EXPERTS, D, 2 * D_FF), D ** -0.5),
        'b_exp1': nrm(26, (L, N_EXPERTS, 2 * D_FF), 0.01),
        'w_exp2': nrm(27, (L, N_EXPERTS, D_FF, D), D_FF ** -0.5),
        'b_exp2': nrm(28, (L, N_EXPERTS, D), 0.01),
        'final_norm_w': 1.0 + nrm(29, (D,), 0.02),
    }


def reference(x, mem, attn_norm_w, mem_norm_w, w_in, b_swa, rwkv_mu, rwkv_w0, rwkv_w2,
              rwkv_a0, rwkv_a2, rwkv_g2, rwkv_kk, rwkv_ka, rwkv_rk, rwkv_lnx_w, rwkv_lnx_b,
              swa_sinks, w_mem_kv, w_gate, w_branch, w_out, ffn_norm_w, w_router, b_router,
              w_exp1, b_exp1, w_exp2, b_exp2, final_norm_w):
    h = x
    for l in range(DEPTH):
        xn = rms_norm(h, attn_norm_w[l])
        mem_n = rms_norm(mem, mem_norm_w[l])
        h = h + hybrid_mixer(xn, mem_n, w_in[l], b_swa[l], rwkv_mu[l], rwkv_w0[l], rwkv_w2[l],
                             rwkv_a0[l], rwkv_a2[l], rwkv_g2[l], rwkv_kk[l], rwkv_ka[l],
                             rwkv_rk[l], rwkv_lnx_w[l], rwkv_lnx_b[l], swa_sinks[l],
                             w_mem_kv[l], w_gate[l], w_branch[l], w_out[l])
        h = h + sparse_moe(rms_norm(h, ffn_norm_w[l]), w_router[l], b_router[l],
                           w_exp1[l], b_exp1[l], w_exp2[l], b_exp2[l])
    return rms_norm(h, final_norm_w)
```

```python
import functools

import jax
import jax.numpy as jnp
from jax import lax
from jax.experimental import pallas as pl
from jax.experimental.pallas import tpu as pltpu

F32 = jnp.float32
BF16 = jnp.bfloat16
I32 = jnp.int32

D_MODEL = 1024
HEAD_DIM = 64
LANES = 128
RWKV_DIM = 512
RWKV_PAIRS = RWKV_DIM // LANES
DECAY_LORA = 64
AAA_LORA = 64
GATE_LORA = 128
RWKV_PROJ = 3 * RWKV_DIM + DECAY_LORA + AAA_LORA + GATE_LORA
LNX_EPS = 64e-5
SWA_Q_HEADS = 8
SWA_KV_HEADS = 2
SWA_GROUP = SWA_Q_HEADS // SWA_KV_HEADS
SWA_DIM = SWA_Q_HEADS * HEAD_DIM
SWA_KV_DIM = SWA_KV_HEADS * HEAD_DIM
SWA_QP = SWA_Q_HEADS * LANES
SWA_KVP = SWA_KV_HEADS * LANES
SWA_PROJ_P = SWA_QP + 2 * SWA_KVP
Q_BLOCK = 128
ALIBI_MAX = 8.0
MEM_HEADS = 4
MEM_HEAD_DIM = 128
MEM_DIM = MEM_HEADS * MEM_HEAD_DIM
N_EXPERTS = 32
TOP_K = 4
D_FF = 1024
SWIGLU_ALPHA = 1.702
SWIGLU_LIMIT = 7.0
EXPERT_BLOCK = 128
NORM_EPS = 1e-5
CHUNK = 64
NEG_BIG = -1e30

VMEM_LIMIT = 56 * 1024 * 1024


def _cparams(*sem):
    return pltpu.CompilerParams(dimension_semantics=sem, vmem_limit_bytes=VMEM_LIMIT)


def _rms(x, w):
    return x * lax.rsqrt(jnp.mean(x * x, axis=-1, keepdims=True) + NORM_EPS) * w


def _dot(a, b):
    return jnp.dot(a, b, preferred_element_type=F32)


def _dot_nt(a, b):
    return lax.dot_general(a, b, (((1,), (1,)), ((), ())), preferred_element_type=F32)


def _split2(x):
    hi = x.astype(BF16)
    lo = (x - hi.astype(F32)).astype(BF16)
    return hi, lo


def _split3(x):
    hi = x.astype(BF16)
    r1 = x - hi.astype(F32)
    mid = r1.astype(BF16)
    lo = (r1 - mid.astype(F32)).astype(BF16)
    return hi, mid, lo


def _group_ones(n, group):
    r = lax.broadcasted_iota(I32, (n, n), 0) // group
    c = lax.broadcasted_iota(I32, (n, n), 1) // group
    return (r == c).astype(BF16)


def _group_sum(x, ones):
    hi, lo = _split2(x)
    return _dot(hi, ones) + _dot(lo, ones)


def _inproj_kernel(x_ref, nw_ref, wr_ref, ws_ref, wq_ref, bs_ref, pr_ref, ps_ref, qm_ref):
    xb = _rms(x_ref[...], nw_ref[...]).astype(BF16)
    pr_ref[...] = _dot(xb, wr_ref[...])
    ps_ref[...] = (_dot(xb, ws_ref[...]) + bs_ref[...]).astype(BF16)
    qm_ref[...] = _dot(xb, wq_ref[...]).astype(BF16)


def _inproj(x2, nw, w_rw, w_sw, w_qm, b_sw, tm):
    n = x2.shape[0]
    const = lambda i: (0, 0)
    row = lambda i: (i, 0)
    return pl.pallas_call(
        _inproj_kernel,
        grid=(n // tm,),
        in_specs=[
            pl.BlockSpec((tm, D_MODEL), row),
            pl.BlockSpec((1, D_MODEL), const),
            pl.BlockSpec((D_MODEL, RWKV_PROJ), const),
            pl.BlockSpec((D_MODEL, SWA_PROJ_P), const),
            pl.BlockSpec((D_MODEL, MEM_DIM), const),
            pl.BlockSpec((1, SWA_PROJ_P), const),
        ],
        out_specs=[
            pl.BlockSpec((tm, RWKV_PROJ), row),
            pl.BlockSpec((tm, SWA_PROJ_P), row),
            pl.BlockSpec((tm, MEM_DIM), row),
        ],
        out_shape=[
            jax.ShapeDtypeStruct((n, RWKV_PROJ), F32),
            jax.ShapeDtypeStruct((n, SWA_PROJ_P), BF16),
            jax.ShapeDtypeStruct((n, MEM_DIM), BF16),
        ],
        compiler_params=_cparams("parallel"),
        name="inproj",
    )(x2, nw, w_rw, w_sw, w_qm, b_sw)


def _memkv_kernel(m_ref, nw_ref, w_ref, o_ref):
    mb = _rms(m_ref[...], nw_ref[...]).astype(BF16)
    o_ref[...] = _dot(mb, w_ref[...]).astype(BF16)


def _memkv(mem2, nw, w, tm):
    n = mem2.shape[0]
    return pl.pallas_call(
        _memkv_kernel,
        grid=(n // tm,),
        in_specs=[
            pl.BlockSpec((tm, D_MODEL), lambda i: (i, 0)),
            pl.BlockSpec((1, D_MODEL), lambda i: (0, 0)),
            pl.BlockSpec((D_MODEL, 2 * MEM_DIM), lambda i: (0, 0)),
        ],
        out_specs=pl.BlockSpec((tm, 2 * MEM_DIM), lambda i: (i, 0)),
        out_shape=jax.ShapeDtypeStruct((n, 2 * MEM_DIM), BF16),
        compiler_params=_cparams("parallel"),
        name="memkv",
    )(mem2, nw, w)


def _rwkv_prep_kernel(p_ref, halo_ref, mu_ref, w0_ref, wwa_ref, a0_ref, g2_ref, kk_ref, ka_ref,
                      rk_ref, r_out, lw_out, k_out, v_out, kk_out, b_out, g_out, bonus_out,
                      *, tiles_per_seq):
    i = pl.program_id(0)
    p = p_ref[...]
    tm = p.shape[0]
    halo = jnp.where(i % tiles_per_seq == 0, 0.0, halo_ref[7:8, :])
    row = lax.broadcasted_iota(I32, p.shape, 0)
    prev = jnp.where(row == 0, halo, pltpu.roll(p, 1, axis=0))
    ps = p + (prev - p) * mu_ref[...]
    r = ps[:, 0:RWKV_DIM]
    k = ps[:, RWKV_DIM:2 * RWKV_DIM]
    v = ps[:, 2 * RWKV_DIM:3 * RWKV_DIM]
    wa = ps[:, 3 * RWKV_DIM:3 * RWKV_DIM + LANES]
    pg = ps[:, 3 * RWKV_DIM + LANES:]
    lane = lax.broadcasted_iota(I32, (tm, LANES), 1)
    z = jnp.where(lane < DECAY_LORA, jnp.tanh(wa), wa).astype(BF16)
    lora = _dot(z, wwa_ref[...])
    w_log = -jax.nn.softplus(-(w0_ref[...] + lora[:, :RWKV_DIM])) - 0.5
    lw_out[...] = -jnp.exp(w_log)
    alr = jax.nn.sigmoid(a0_ref[...] + lora[:, RWKV_DIM:])
    g_out[...] = _dot(jax.nn.sigmoid(pg).astype(BF16), g2_ref[...])
    ones = _group_ones(RWKV_DIM, HEAD_DIM)
    kk = k * kk_ref[...]
    kk = kk / jnp.maximum(jnp.sqrt(_group_sum(kk * kk, ones)), 1e-12)
    k2 = k * (1.0 + (alr - 1.0) * ka_ref[...])
    r_out[...] = r
    k_out[...] = k2
    v_out[...] = v
    kk_out[...] = kk
    b_out[...] = kk * alr
    bonus_out[...] = _group_sum(r * k2 * rk_ref[...], ones) * v


def _rwkv_prep(p_rw, mu, w0, wwa, a0, g2, k_k, k_a, r_k, tm, seq):
    n = p_rw.shape[0]
    tiles_per_seq = seq // tm
    const = lambda i: (0, 0)
    row = lambda i: (i, 0)
    vec = pl.BlockSpec((1, RWKV_DIM), const)
    out = pl.BlockSpec((tm, RWKV_DIM), row)
    return pl.pallas_call(
        functools.partial(_rwkv_prep_kernel, tiles_per_seq=tiles_per_seq),
        grid=(n // tm,),
        in_specs=[
            pl.BlockSpec((tm, RWKV_PROJ), row),
            pl.BlockSpec((8, RWKV_PROJ), lambda i: (jnp.maximum(i * (tm // 8) - 1, 0), 0)),
            pl.BlockSpec((1, RWKV_PROJ), const),
            vec,
            pl.BlockSpec((LANES, 2 * RWKV_DIM), const),
            vec,
            pl.BlockSpec((GATE_LORA, RWKV_DIM), const),
            vec, vec, vec,
        ],
        out_specs=[out] * 8,
        out_shape=[jax.ShapeDtypeStruct((n, RWKV_DIM), F32)] * 8,
        compiler_params=_cparams("parallel"),
        name="rwkv_prep",
    )(p_rw, p_rw, mu, w0, wwa, a0, g2, k_k, k_a, r_k)


def _blockdiag(x, first_head):
    zero = jnp.zeros_like(x)
    return jnp.concatenate([jnp.where(first_head, x, zero), jnp.where(first_head, zero, x)], axis=0)


def _rwkv_scan_kernel(r_ref, lw_ref, k_ref, v_ref, kk_ref, b_ref, g_ref, bonus_ref, lnw_ref,
                      lnb_ref, y_ref, s_ref, *, chunks):
    @pl.when(pl.program_id(1) == 0)
    def _():
        s_ref[...] = jnp.zeros_like(s_ref)

    c2 = 2 * CHUNK
    tri_incl = (lax.broadcasted_iota(I32, (CHUNK, CHUNK), 1)
                <= lax.broadcasted_iota(I32, (CHUNK, CHUNK), 0)).astype(BF16)
    rowi = lax.broadcasted_iota(I32, (c2, c2), 0)
    coli = lax.broadcasted_iota(I32, (c2, c2), 1)
    strict = coli < rowi
    incl = coli <= rowi
    eye = (coli == rowi).astype(F32)
    first_head = lax.broadcasted_iota(I32, (CHUNK, LANES), 1) < HEAD_DIM
    ones = _group_ones(RWKV_DIM, HEAD_DIM)
    inv_n = 1.0 / HEAD_DIM

    def chunk_body(c, carry):
        rows = pl.ds(pl.multiple_of(c * CHUNK, CHUNK), CHUNK)
        lw = lw_ref[rows, :]
        hi, mid, lo = _split3(lw)
        cl = _dot(tri_incl, hi) + _dot(tri_incl, mid) + _dot(tri_incl, lo)
        cl_end = cl[CHUNK - 1:CHUNK, :]
        r = r_ref[rows, :]
        k = k_ref[rows, :]
        v = v_ref[rows, :]
        kk = kk_ref[rows, :]
        b = b_ref[rows, :]
        inv = jnp.exp(-cl)
        to_end = jnp.exp(cl_end - cl)
        a_t = -kk * jnp.exp(cl - lw)
        r_t = r * jnp.exp(cl)
        b_t = b * inv
        k_t = k * inv
        b_h = b * to_end
        k_h = k * to_end
        p_end = jnp.exp(cl_end)
        ys = []
        for j in range(RWKV_PAIRS):
            ls = slice(j * LANES, (j + 1) * LANES)
            bd = lambda x: _blockdiag(x[:, ls], first_head).astype(BF16)
            a2, r2, b2, k2, v2, bh2, kh2 = (bd(x) for x in (a_t, r_t, b_t, k_t, v, b_h, k_h))
            lhs = jnp.concatenate([a2, r2], axis=0)
            a4 = _dot_nt(lhs, jnp.concatenate([b2, k2], axis=0))
            a_ab = jnp.where(strict, a4[:c2, :c2], 0.0)
            a_ak = jnp.where(strict, a4[:c2, c2:], 0.0)
            a_rb = jnp.where(incl, a4[c2:, :c2], 0.0)
            a_rk = jnp.where(incl, a4[c2:, c2:], 0.0)
            tinv = eye + a_ab
            apow = a_ab
            for _ in range(5):
                ab = apow.astype(BF16)
                apow = _dot(ab, ab)
                tinv = tinv + _dot(tinv.astype(BF16), apow.astype(BF16))
            s2 = s_ref[j]
            u = _dot_nt(lhs, s2.astype(BF16))
            sa = _dot(tinv.astype(BF16), (u[:c2] + _dot(a_ak.astype(BF16), v2)).astype(BF16))
            sav = jnp.concatenate([sa.astype(BF16), v2], axis=0)
            y2 = u[c2:] + _dot(jnp.concatenate([a_rb, a_rk], axis=1).astype(BF16), sav)
            sav_t = jnp.concatenate([sa, v2.astype(F32)], axis=0).T.astype(BF16)
            s_ref[j] = s2 * p_end[:, ls] + _dot(sav_t, jnp.concatenate([bh2, kh2], axis=0))
            ys.append(y2[:CHUNK] + y2[CHUNK:])
        y = jnp.concatenate(ys, axis=1)
        m = _group_sum(y, ones) * inv_n
        d = y - m
        var = _group_sum(d * d, ones) * inv_n
        yn = d * lax.rsqrt(var + LNX_EPS) * lnw_ref[...] + lnb_ref[...]
        y_ref[rows, :] = ((yn + bonus_ref[rows, :]) * g_ref[rows, :]).astype(BF16)
        return carry

    lax.fori_loop(0, chunks, chunk_body, 0)


def _rwkv_scan(r, lw, k, v, kk, b, g, bonus, lnw, lnb, batch, seq, tc):
    n = r.shape[0]
    steps = seq // tc
    row = lambda bi, ti: (bi * steps + ti, 0)
    const = lambda bi, ti: (0, 0)
    blk = pl.BlockSpec((tc, RWKV_DIM), row)
    vec = pl.BlockSpec((1, RWKV_DIM), const)
    return pl.pallas_call(
        functools.partial(_rwkv_scan_kernel, chunks=tc // CHUNK),
        grid=(batch, steps),
        in_specs=[blk] * 8 + [vec, vec],
        out_specs=blk,
        out_shape=jax.ShapeDtypeStruct((n, RWKV_DIM), BF16),
        scratch_shapes=[pltpu.VMEM((RWKV_PAIRS, LANES, LANES), F32)],
        compiler_params=_cparams("parallel", "arbitrary"),
        name="rwkv_scan",
    )(r, lw, k, v, kk, b, g, bonus, lnw, lnb)


def _swa_kernel(sink_ref, q_ref, kvc_ref, kvp_ref, o_ref):
    nblk = pl.program_id(1)
    kv = jnp.concatenate([kvp_ref[...], kvc_ref[...]], axis=0)
    i = lax.broadcasted_iota(I32, (Q_BLOCK, 2 * Q_BLOCK), 0)
    j = lax.broadcasted_iota(I32, (Q_BLOCK, 2 * Q_BLOCK), 1)
    dist = i + Q_BLOCK - j
    valid = (dist >= 0) & (dist < Q_BLOCK) & ((j >= Q_BLOCK) | (nblk > 0))
    distf = dist.astype(F32)
    for h in range(SWA_KV_HEADS):
        kh = kv[:, h * LANES:(h + 1) * LANES]
        vh = kv[:, SWA_KVP + h * LANES:SWA_KVP + (h + 1) * LANES]
        for gi in range(SWA_GROUP):
            hq = h * SWA_GROUP + gi
            cols = slice(hq * LANES, (hq + 1) * LANES)
            slope = 2.0 ** (-ALIBI_MAX * (hq + 1) / SWA_Q_HEADS)
            s = _dot_nt(q_ref[:, cols], kh) - slope * distf
            s = jnp.where(valid, s, NEG_BIG)
            sink = sink_ref[hq]
            m = jnp.maximum(jnp.max(s, axis=-1, keepdims=True), sink)
            e = jnp.exp(s - m)
            den = jnp.sum(e, axis=-1, keepdims=True) + jnp.exp(sink - m)
            o_ref[:, cols] = _dot((e / den).astype(BF16), vh).astype(BF16)


def _swa(p_sw, sinks, batch, seq):
    n = p_sw.shape[0]
    nb = seq // Q_BLOCK
    kv_col = SWA_QP // (2 * SWA_KVP)
    return pl.pallas_call(
        _swa_kernel,
        grid=(batch, nb),
        in_specs=[
            pl.BlockSpec(memory_space=pltpu.SMEM),
            pl.BlockSpec((Q_BLOCK, SWA_QP), lambda bi, ni: (bi * nb + ni, 0)),
            pl.BlockSpec((Q_BLOCK, 2 * SWA_KVP), lambda bi, ni: (bi * nb + ni, kv_col)),
            pl.BlockSpec((Q_BLOCK, 2 * SWA_KVP),
                         lambda bi, ni: (bi * nb + jnp.maximum(ni - 1, 0), kv_col)),
        ],
        out_specs=pl.BlockSpec((Q_BLOCK, SWA_QP), lambda bi, ni: (bi * nb + ni, 0)),
        out_shape=jax.ShapeDtypeStruct((n, SWA_QP), BF16),
        compiler_params=_cparams("parallel", "parallel"),
        name="swa",
    )(sinks, p_sw, p_sw, p_sw)


def _merge_kernel(x_ref, yr_ref, ys_ref, qm_ref, km_ref, vm_ref, anw_ref, wg_ref, wb0_ref, wb1_ref,
                  wb2_ref, wo_ref, fnw_ref, wrh_ref, wrl_ref, br_ref,
                  h_ref, hn_ref, meta_ref, cnt_ref, carry_ref):
    @pl.when(pl.program_id(0) == 0)
    def _():
        carry_ref[...] = jnp.zeros_like(carry_ref)

    x = x_ref[...]
    tm = x.shape[0]
    xb = _rms(x, anw_ref[...]).astype(BF16)
    outs = []
    for hh in range(MEM_HEADS):
        cols = slice(hh * MEM_HEAD_DIM, (hh + 1) * MEM_HEAD_DIM)
        s = _dot_nt(qm_ref[:, cols], km_ref[:, cols]) * (MEM_HEAD_DIM ** -0.5)
        m = jnp.max(s, axis=-1, keepdims=True)
        e = jnp.exp(s - m)
        pr = (e / jnp.sum(e, axis=-1, keepdims=True)).astype(BF16)
        outs.append(_dot(pr, vm_ref[:, cols]))
    y_mem = jnp.concatenate(outs, axis=1).astype(BF16)
    merged = jax.nn.sigmoid(_dot(xb, wg_ref[0])) * _dot(yr_ref[...], wb0_ref[...])
    merged += jax.nn.sigmoid(_dot(xb, wg_ref[1])) * _dot(ys_ref[...], wb1_ref[...])
    merged += jax.nn.sigmoid(_dot(xb, wg_ref[2])) * _dot(y_mem, wb2_ref[...])
    h = x + _dot(merged.astype(BF16), wo_ref[...])
    h_ref[...] = h
    hn = _rms(h, fnw_ref[...])
    hn_ref[...] = hn
    hi, lo = _split2(hn)
    logits = _dot(hi, wrh_ref[...]) + _dot(hi, wrl_ref[...]) + _dot(lo, wrh_ref[...]) + br_ref[...]
    lane = lax.broadcasted_iota(I32, (tm, LANES), 1).astype(F32)
    work = logits
    vals, idxs, sels = [], [], []
    for _ in range(TOP_K):
        mk = jnp.max(work, axis=-1, keepdims=True)
        ik = jnp.min(jnp.where(work == mk, lane, float(LANES)), axis=-1, keepdims=True)
        sel = lane == ik
        work = jnp.where(sel, 2.0 * NEG_BIG, work)
        vals.append(mk)
        idxs.append(ik)
        sels.append(sel)
    es = [jnp.exp(vk - vals[0]) for vk in vals]
    den = es[0] + es[1] + es[2] + es[3]
    cnt = jnp.zeros((tm, LANES), F32)
    for sel in sels:
        cnt = cnt + sel.astype(F32)
    tri = (lax.broadcasted_iota(I32, (tm, tm), 1) < lax.broadcasted_iota(I32, (tm, tm), 0)).astype(BF16)
    prefix = _dot(tri, cnt.astype(BF16)) + carry_ref[0:1, :]
    meta = jnp.zeros((tm, LANES), F32)
    for kk in range(TOP_K):
        pos = jnp.sum(jnp.where(sels[kk], prefix, 0.0), axis=-1, keepdims=True)
        meta = jnp.where(lane == kk, idxs[kk], meta)
        meta = jnp.where(lane == TOP_K + kk, es[kk] / den, meta)
        meta = jnp.where(lane == 2 * TOP_K + kk, pos, meta)
    meta_ref[...] = meta
    total = carry_ref[0:1, :] + jnp.sum(cnt, axis=0, keepdims=True)
    carry_ref[...] = jnp.broadcast_to(total, carry_ref.shape)
    cnt_ref[...] = jnp.broadcast_to(total, cnt_ref.shape)


def _merge(x2, y_rw, y_sw, q_mem, memkv, anw, wg, wb0, wb1, wb2, wo, fnw, wrh, wrl, br, tm, seq,
           mem_len):
    n = x2.shape[0]
    tiles_per_seq = seq // tm
    row = lambda i: (i, 0)
    const = lambda i: (0, 0)
    return pl.pallas_call(
        _merge_kernel,
        grid=(n // tm,),
        in_specs=[
            pl.BlockSpec((tm, D_MODEL), row),
            pl.BlockSpec((tm, RWKV_DIM), row),
            pl.BlockSpec((tm, SWA_QP), row),
            pl.BlockSpec((tm, MEM_DIM), row),
            pl.BlockSpec((mem_len, MEM_DIM), lambda i: (i // tiles_per_seq, 0)),
            pl.BlockSpec((mem_len, MEM_DIM), lambda i: (i // tiles_per_seq, 1)),
            pl.BlockSpec((1, D_MODEL), const),
            pl.BlockSpec((3, D_MODEL, D_MODEL), lambda i: (0, 0, 0)),
            pl.BlockSpec((RWKV_DIM, D_MODEL), const),
            pl.BlockSpec((SWA_QP, D_MODEL), const),
            pl.BlockSpec((MEM_DIM, D_MODEL), const),
            pl.BlockSpec((D_MODEL, D_MODEL), const),
            pl.BlockSpec((1, D_MODEL), const),
            pl.BlockSpec((D_MODEL, LANES), const),
            pl.BlockSpec((D_MODEL, LANES), const),
            pl.BlockSpec((1, LANES), const),
        ],
        out_specs=[
            pl.BlockSpec((tm, D_MODEL), row),
            pl.BlockSpec((tm, D_MODEL), row),
            pl.BlockSpec((tm, LANES), row),
            pl.BlockSpec((8, LANES), const),
        ],
        out_shape=[
            jax.ShapeDtypeStruct((n, D_MODEL), F32),
            jax.ShapeDtypeStruct((n, D_MODEL), F32),
            jax.ShapeDtypeStruct((n, LANES), F32),
            jax.ShapeDtypeStruct((8, LANES), F32),
        ],
        scratch_shapes=[pltpu.VMEM((8, LANES), F32)],
        compiler_params=_cparams("arbitrary"),
        name="merge_router",
    )(x2, y_rw, y_sw, q_mem, memkv, memkv, anw, wg, wb0, wb1, wb2, wo, fnw, wrh, wrl, br)


def _row_copy(src_hbm, dst_hbm, idx_smem, xbuf, ybuf, sem, j, gather):
    if gather:
        return pltpu.make_async_copy(src_hbm.at[pl.ds(idx_smem[j], 1), :],
                                     xbuf.at[pl.ds(j, 1), :], sem.at[0])
    return pltpu.make_async_copy(ybuf.at[pl.ds(j, 1), :],
                                 dst_hbm.at[pl.ds(idx_smem[EXPERT_BLOCK + j], 1), :], sem.at[1])


def _moe_kernel(be_ref, nused_ref, idx_hbm, hn_hbm, w1g_ref, w1l_ref, b1g_ref, b1l_ref, w2_ref,
                b2_ref, y_hbm, idx_smem, xbuf, ybuf, sem, isem):
    i = pl.program_id(0)

    @pl.when(i == 0)
    def _():
        ybuf[...] = jnp.zeros_like(ybuf)
        dump = pltpu.make_async_copy(
            ybuf, y_hbm.at[pl.ds(y_hbm.shape[0] - EXPERT_BLOCK, EXPERT_BLOCK), :], sem.at[1])
        dump.start()
        dump.wait()

    @pl.when(i < nused_ref[0])
    def _():
        icp =pltpu.make_async_copy(idx_hbm.at[i], idx_smem, isem.at[0])
        icp.start()
        icp.wait()
        for j in range(EXPERT_BLOCK):
            _row_copy(hn_hbm, y_hbm, idx_smem, xbuf, ybuf, sem, j, True).start()
        for j in range(EXPERT_BLOCK):
            _row_copy(hn_hbm, y_hbm, idx_smem, xbuf, ybuf, sem, j, True).wait()
        xb = xbuf[...].astype(BF16)
        glu = _dot(xb, w1g_ref[...]) + b1g_ref[...]
        lin = _dot(xb, w1l_ref[...]) + b1l_ref[...]
        glu = jnp.minimum(glu, SWIGLU_LIMIT)
        lin = jnp.clip(lin, -SWIGLU_LIMIT, SWIGLU_LIMIT)
        act = glu * jax.nn.sigmoid(SWIGLU_ALPHA * glu) * (lin + 1.0)
        ybuf[...] = _dot(act.astype(BF16), w2_ref[...]) + b2_ref[...]
        for j in range(EXPERT_BLOCK):
            _row_copy(hn_hbm, y_hbm, idx_smem, xbuf, ybuf, sem, j, False).start()
        for j in range(EXPERT_BLOCK):
            _row_copy(hn_hbm, y_hbm, idx_smem, xbuf, ybuf, sem, j, False).wait()


def _moe_ffn(block_e, n_used, slot_idx, hn, w1g, w1l, b1g, b1l, w2, b2, n_rows_out):
    n_blocks = slot_idx.shape[0]
    wspec = pl.BlockSpec((None, D_MODEL, D_FF), lambda i, be, nu: (be[i], 0, 0))
    bspec = pl.BlockSpec((None, 1, D_FF), lambda i, be, nu: (be[i], 0, 0))
    grid_spec = pltpu.PrefetchScalarGridSpec(
        num_scalar_prefetch=2,
        grid=(n_blocks,),
        in_specs=[
            pl.BlockSpec(memory_space=pl.ANY),
            pl.BlockSpec(memory_space=pl.ANY),
            wspec, wspec, bspec, bspec,
            pl.BlockSpec((None, D_FF, D_MODEL), lambda i, be, nu: (be[i], 0, 0)),
            pl.BlockSpec((None, 1, D_MODEL), lambda i, be, nu: (be[i], 0, 0)),
        ],
        out_specs=pl.BlockSpec(memory_space=pl.ANY),
        scratch_shapes=[
            pltpu.SMEM((2 * EXPERT_BLOCK,), I32),
            pltpu.VMEM((EXPERT_BLOCK, D_MODEL), F32),
            pltpu.VMEM((EXPERT_BLOCK, D_MODEL), F32),
            pltpu.SemaphoreType.DMA((2,)),
            pltpu.SemaphoreType.DMA((1,)),
        ],
    )
    return pl.pallas_call(
        _moe_kernel,
        grid_spec=grid_spec,
        out_shape=jax.ShapeDtypeStruct((n_rows_out, D_MODEL), F32),
        compiler_params=_cparams("arbitrary"),
        name="moe_ffn",
    )(block_e, n_used, slot_idx, hn, w1g, w1l, b1g, b1l, w2, b2)


def _combine_kernel(h_ref, y_ref, meta_ref, fw_ref, o_ref):
    acc = h_ref[...]
    meta = meta_ref[...]
    for kk in range(TOP_K):
        gate = meta[:, TOP_K + kk:TOP_K + kk + 1]
        acc = acc + gate * y_ref[:, kk * D_MODEL:(kk + 1) * D_MODEL]
    o_ref[...] = _rms(acc, fw_ref[...])


def _combine(h, y4, meta, fw, tm):
    n = h.shape[0]
    row = lambda i: (i, 0)
    return pl.pallas_call(
        _combine_kernel,
        grid=(n // tm,),
        in_specs=[
            pl.BlockSpec((tm, D_MODEL), row),
            pl.BlockSpec((tm, TOP_K * D_MODEL), row),
            pl.BlockSpec((tm, LANES), row),
            pl.BlockSpec((1, D_MODEL), lambda i: (0, 0)),
        ],
        out_specs=pl.BlockSpec((tm, D_MODEL), row),
        out_shape=jax.ShapeDtypeStruct((n, D_MODEL), F32),
        compiler_params=_cparams("parallel"),
        name="combine",
    )(h, y4, meta, fw)


def _pad_heads_cols(w, heads):
    lead = w.shape[:-1]
    w = w.reshape(lead + (heads, HEAD_DIM))
    w = jnp.pad(w, [(0, 0)] * len(lead) + [(0, 0), (0, LANES - HEAD_DIM)])
    return w.reshape(lead + (heads * LANES,))


def _pick_tile(n, pref):
    t = pref
    while n % t:
        t //= 2
    return t


def kernel(x, mem, attn_norm_w, mem_norm_w, w_in, b_swa, rwkv_mu, rwkv_w0, rwkv_w2, rwkv_a0,
           rwkv_a2, rwkv_g2, rwkv_kk, rwkv_ka, rwkv_rk, rwkv_lnx_w, rwkv_lnx_b, swa_sinks,
           w_mem_kv, w_gate, w_branch, w_out, ffn_norm_w, w_router, b_router, w_exp1, b_exp1,
           w_exp2, b_exp2, final_norm_w):
    batch, seq, d = x.shape
    mem_len = mem.shape[1]
    n = batch * seq
    layer = 0
    x2 = x.reshape(n, d)
    row = lambda a: a.reshape(1, -1)

    w_in_l = w_in[layer]
    w_rw = w_in_l[:, :RWKV_PROJ].astype(BF16)
    sw = w_in_l[:, RWKV_PROJ:RWKV_PROJ + SWA_DIM + 2 * SWA_KV_DIM]
    bs = b_swa[layer]
    scale = HEAD_DIM ** -0.5
    w_sw = jnp.concatenate([
        _pad_heads_cols(sw[:, :SWA_DIM] * scale, SWA_Q_HEADS),
        _pad_heads_cols(sw[:, SWA_DIM:SWA_DIM + SWA_KV_DIM], SWA_KV_HEADS),
        _pad_heads_cols(sw[:, SWA_DIM + SWA_KV_DIM:], SWA_KV_HEADS)], axis=1).astype(BF16)
    b_sw = row(jnp.concatenate([
        _pad_heads_cols(bs[:SWA_DIM] * scale, SWA_Q_HEADS),
        _pad_heads_cols(bs[SWA_DIM:SWA_DIM + SWA_KV_DIM], SWA_KV_HEADS),
        _pad_heads_cols(bs[SWA_DIM + SWA_KV_DIM:], SWA_KV_HEADS)]))
    w_qm = w_in_l[:, RWKV_PROJ + SWA_DIM + 2 * SWA_KV_DIM:].astype(BF16)
    zeros_l = jnp.zeros((DECAY_LORA, RWKV_DIM), F32)
    wwa = jnp.concatenate([
        jnp.concatenate([rwkv_w2[layer], zeros_l], axis=1),
        jnp.concatenate([zeros_l, rwkv_a2[layer]], axis=1)], axis=0).astype(BF16)
    wb = w_branch[layer]
    wb1 = _pad_heads_cols(wb[1].T, SWA_Q_HEADS).T.astype(BF16)
    wr = jnp.pad(w_router[layer], ((0, 0), (0, LANES - N_EXPERTS)))
    wrh = wr.astype(BF16)
    wrl = (wr - wrh.astype(F32)).astype(BF16)
    br = row(jnp.pad(b_router[layer], (0, LANES - N_EXPERTS), constant_values=NEG_BIG))
    w1 = w_exp1[layer]
    b1 = b_exp1[layer]
    w1g = w1[:, :, 0::2].astype(BF16)
    w1l = w1[:, :, 1::2].astype(BF16)
    b1g = b1[:, None, 0::2]
    b1l = b1[:, None, 1::2]
    w2 = w_exp2[layer].astype(BF16)
    b2 = b_exp2[layer][:, None, :]

    tm_proj = _pick_tile(seq, 512)
    p_rw, p_sw, q_mem = _inproj(x2, row(attn_norm_w[layer]), w_rw, w_sw, w_qm, b_sw, tm_proj)
    memkv = _memkv(mem.reshape(batch * mem_len, d), row(mem_norm_w[layer]),
                   w_mem_kv[layer].astype(BF16), mem_len)
    r, lw, k, v, kk, b, g, bonus = _rwkv_prep(
        p_rw, row(rwkv_mu[layer]), row(rwkv_w0[layer]), wwa, row(rwkv_a0[layer]),
        rwkv_g2[layer].astype(BF16), row(rwkv_kk[layer]), row(rwkv_ka[layer]),
        row(rwkv_rk[layer]), tm_proj, seq)
    y_rw = _rwkv_scan(r, lw, k, v, kk, b, g, bonus, row(rwkv_lnx_w[layer]),
                      row(rwkv_lnx_b[layer]), batch, seq, _pick_tile(seq, 256))
    y_sw = _swa(p_sw, swa_sinks[layer], batch, seq)

    tm_merge = _pick_tile(seq, 256)
    h, hn, meta, counts = _merge(
        x2, y_rw, y_sw, q_mem, memkv, row(attn_norm_w[layer]), w_gate[layer].astype(BF16),
        wb[0].astype(BF16), wb1, wb[2].astype(BF16), w_out[layer].astype(BF16),
        row(ffn_norm_w[layer]), wrh, wrl, br, tm_merge, seq, mem_len)

    a_total = n * TOP_K
    e_idx = meta[:, 0:TOP_K].astype(I32)
    pos = meta[:, 2 * TOP_K:3 * TOP_K].astype(I32)
    cnt = counts[0, :N_EXPERTS].astype(I32)
    padded = ((cnt + EXPERT_BLOCK - 1) // EXPERT_BLOCK) * EXPERT_BLOCK
    pad_ends = jnp.cumsum(padded)
    pad_starts = pad_ends - padded
    n_blocks = -(-a_total // EXPERT_BLOCK) + N_EXPERTS
    p_slots = n_blocks * EXPERT_BLOCK
    dest = (pad_starts[e_idx] + pos).reshape(-1)
    a_ids = jnp.arange(a_total, dtype=I32)
    slot_a = jnp.full((p_slots,), -1, I32).at[dest].set(a_ids)
    lane_id = jnp.arange(p_slots, dtype=I32) % EXPERT_BLOCK
    slot_tok = jnp.where(slot_a >= 0, slot_a // TOP_K, 0)
    slot_dst = jnp.where(slot_a >= 0, slot_a, a_total + lane_id)
    slot_idx = jnp.concatenate([slot_tok.reshape(n_blocks, EXPERT_BLOCK),
                                slot_dst.reshape(n_blocks, EXPERT_BLOCK)], axis=1)
    block_e = jnp.minimum(
        jnp.searchsorted(pad_ends, jnp.arange(n_blocks, dtype=I32) * EXPERT_BLOCK, side='right'),
        N_EXPERTS - 1).astype(I32)
    n_used = (pad_ends[-1:] // EXPERT_BLOCK).astype(I32)

    y_flat = _moe_ffn(block_e, n_used, slot_idx, hn, w1g, w1l, b1g, b1l, w2, b2,
                      a_total + EXPERT_BLOCK)
    y4 = y_flat.reshape(-1, TOP_K * D_MODEL)
    out = _combine(h, y4, meta, row(final_norm_w), tm_merge)
    return out.reshape(batch, seq, d)
```

```python
import functools

import jax
import jax.numpy as jnp
from jax import lax
from jax.experimental import pallas as pl
from jax.experimental.pallas import tpu as pltpu

F32 = jnp.float32
BF16 = jnp.bfloat16
I32 = jnp.int32

D_MODEL = 1024
HEAD_DIM = 64
LANES = 128
RWKV_DIM = 512
RWKV_PAIRS = RWKV_DIM // LANES
DECAY_LORA = 64
AAA_LORA = 64
GATE_LORA = 128
RWKV_PROJ = 3 * RWKV_DIM + DECAY_LORA + AAA_LORA + GATE_LORA
LNX_EPS = 64e-5
SWA_Q_HEADS = 8
SWA_KV_HEADS = 2
SWA_GROUP = SWA_Q_HEADS // SWA_KV_HEADS
SWA_DIM = SWA_Q_HEADS * HEAD_DIM
SWA_KV_DIM = SWA_KV_HEADS * HEAD_DIM
SWA_QP = SWA_Q_HEADS * LANES
SWA_KVP = SWA_KV_HEADS * LANES
SWA_PROJ_P = SWA_QP + 2 * SWA_KVP
Q_BLOCK = 128
ALIBI_MAX = 8.0
MEM_HEADS = 4
MEM_HEAD_DIM = 128
MEM_DIM = MEM_HEADS * MEM_HEAD_DIM
N_EXPERTS = 32
TOP_K = 4
D_FF = 1024
SWIGLU_ALPHA = 1.702
SWIGLU_LIMIT = 7.0
EXPERT_BLOCK = 128
NORM_EPS = 1e-5
CHUNK = 64
NEG_BIG = -1e30

VMEM_LIMIT = 56 * 1024 * 1024


def _cparams(*sem):
    return pltpu.CompilerParams(dimension_semantics=sem, vmem_limit_bytes=VMEM_LIMIT)


def _rms(x, w):
    return x * lax.rsqrt(jnp.mean(x * x, axis=-1, keepdims=True) + NORM_EPS) * w


def _dot(a, b):
    return jnp.dot(a, b, preferred_element_type=F32)


def _dot_nt(a, b):
    return lax.dot_general(a, b, (((1,), (1,)), ((), ())), preferred_element_type=F32)


def _split2(x):
    hi = x.astype(BF16)
    lo = (x - hi.astype(F32)).astype(BF16)
    return hi, lo


def _split3(x):
    hi = x.astype(BF16)
    r1 = x - hi.astype(F32)
    mid = r1.astype(BF16)
    lo = (r1 - mid.astype(F32)).astype(BF16)
    return hi, mid, lo


def _group_ones(n, group):
    r = lax.broadcasted_iota(I32, (n, n), 0) // group
    c = lax.broadcasted_iota(I32, (n, n), 1) // group
    return (r == c).astype(BF16)


def _group_sum(x, ones):
    hi, lo = _split2(x)
    return _dot(hi, ones) + _dot(lo, ones)


def _inproj_kernel(x_ref, nw_ref, wr_ref, ws_ref, wq_ref, bs_ref, pr_ref, ps_ref, qm_ref):
    xb = _rms(x_ref[...], nw_ref[...]).astype(BF16)
    pr_ref[...] = _dot(xb, wr_ref[...])
    ps_ref[...] = (_dot(xb, ws_ref[...]) + bs_ref[...]).astype(BF16)
    qm_ref[...] = _dot(xb, wq_ref[...]).astype(BF16)


def _inproj(x2, nw, w_rw, w_sw, w_qm, b_sw, tm):
    n = x2.shape[0]
    const = lambda i: (0, 0)
    row = lambda i: (i, 0)
    return pl.pallas_call(
        _inproj_kernel,
        grid=(n // tm,),
        in_specs=[
            pl.BlockSpec((tm, D_MODEL), row),
            pl.BlockSpec((1, D_MODEL), const),
            pl.BlockSpec((D_MODEL, RWKV_PROJ), const),
            pl.BlockSpec((D_MODEL, SWA_PROJ_P), const),
            pl.BlockSpec((D_MODEL, MEM_DIM), const),
            pl.BlockSpec((1, SWA_PROJ_P), const),
        ],
        out_specs=[
            pl.BlockSpec((tm, RWKV_PROJ), row),
            pl.BlockSpec((tm, SWA_PROJ_P), row),
            pl.BlockSpec((tm, MEM_DIM), row),
        ],
        out_shape=[
            jax.ShapeDtypeStruct((n, RWKV_PROJ), F32),
            jax.ShapeDtypeStruct((n, SWA_PROJ_P), BF16),
            jax.ShapeDtypeStruct((n, MEM_DIM), BF16),
        ],
        compiler_params=_cparams("parallel"),
        name="inproj",
    )(x2, nw, w_rw, w_sw, w_qm, b_sw)


def _memkv_kernel(m_ref, nw_ref, w_ref, o_ref):
    mb = _rms(m_ref[...], nw_ref[...]).astype(BF16)
    o_ref[...] = _dot(mb, w_ref[...]).astype(BF16)


def _memkv(mem2, nw, w, tm):
    n = mem2.shape[0]
    return pl.pallas_call(
        _memkv_kernel,
        grid=(n // tm,),
        in_specs=[
            pl.BlockSpec((tm, D_MODEL), lambda i: (i, 0)),
            pl.BlockSpec((1, D_MODEL), lambda i: (0, 0)),
            pl.BlockSpec((D_MODEL, 2 * MEM_DIM), lambda i: (0, 0)),
        ],
        out_specs=pl.BlockSpec((tm, 2 * MEM_DIM), lambda i: (i, 0)),
        out_shape=jax.ShapeDtypeStruct((n, 2 * MEM_DIM), BF16),
        compiler_params=_cparams("parallel"),
        name="memkv",
    )(mem2, nw, w)


def _rwkv_prep_kernel(p_ref, halo_ref, mu_ref, w0_ref, wwa_ref, a0_ref, g2_ref, kk_ref, ka_ref,
                      rk_ref, r_out, lw_out, k_out, v_out, kk_out, b_out, g_out, bonus_out,
                      *, tiles_per_seq):
    i = pl.program_id(0)
    p = p_ref[...]
    tm = p.shape[0]
    halo = jnp.where(i % tiles_per_seq == 0, 0.0, halo_ref[7:8, :])
    row = lax.broadcasted_iota(I32, p.shape, 0)
    prev = jnp.where(row == 0, halo, pltpu.roll(p, 1, axis=0))
    ps = p + (prev - p) * mu_ref[...]
    r = ps[:, 0:RWKV_DIM]
    k = ps[:, RWKV_DIM:2 * RWKV_DIM]
    v = ps[:, 2 * RWKV_DIM:3 * RWKV_DIM]
    wa = ps[:, 3 * RWKV_DIM:3 * RWKV_DIM + LANES]
    pg = ps[:, 3 * RWKV_DIM + LANES:]
    lane = lax.broadcasted_iota(I32, (tm, LANES), 1)
    z = jnp.where(lane < DECAY_LORA, jnp.tanh(wa), wa).astype(BF16)
    lora = _dot(z, wwa_ref[...])
    w_log = -jax.nn.softplus(-(w0_ref[...] + lora[:, :RWKV_DIM])) - 0.5
    lw_out[...] = -jnp.exp(w_log)
    alr = jax.nn.sigmoid(a0_ref[...] + lora[:, RWKV_DIM:])
    g_out[...] = _dot(jax.nn.sigmoid(pg).astype(BF16), g2_ref[...])
    ones = _group_ones(RWKV_DIM, HEAD_DIM)
    kk = k * kk_ref[...]
    kk = kk / jnp.maximum(jnp.sqrt(_group_sum(kk * kk, ones)), 1e-12)
    k2 = k * (1.0 + (alr - 1.0) * ka_ref[...])
    r_out[...] = r
    k_out[...] = k2
    v_out[...] = v
    kk_out[...] = kk
    b_out[...] = kk * alr
    bonus_out[...] = _group_sum(r * k2 * rk_ref[...], ones) * v


def _rwkv_prep(p_rw, mu, w0, wwa, a0, g2, k_k, k_a, r_k, tm, seq):
    n = p_rw.shape[0]
    tiles_per_seq = seq // tm
    const = lambda i: (0, 0)
    row = lambda i: (i, 0)
    vec = pl.BlockSpec((1, RWKV_DIM), const)
    out = pl.BlockSpec((tm, RWKV_DIM), row)
    return pl.pallas_call(
        functools.partial(_rwkv_prep_kernel, tiles_per_seq=tiles_per_seq),
        grid=(n // tm,),
        in_specs=[
            pl.BlockSpec((tm, RWKV_PROJ), row),
            pl.BlockSpec((8, RWKV_PROJ), lambda i: (jnp.maximum(i * (tm // 8) - 1, 0), 0)),
            pl.BlockSpec((1, RWKV_PROJ), const),
            vec,
            pl.BlockSpec((LANES, 2 * RWKV_DIM), const),
            vec,
            pl.BlockSpec((GATE_LORA, RWKV_DIM), const),
            vec, vec, vec,
        ],
        out_specs=[out] * 8,
        out_shape=[jax.ShapeDtypeStruct((n, RWKV_DIM), F32)] * 8,
        compiler_params=_cparams("parallel"),
        name="rwkv_prep",
    )(p_rw, p_rw, mu, w0, wwa, a0, g2, k_k, k_a, r_k)


def _blockdiag(x, first_head):
    zero = jnp.zeros_like(x)
    return jnp.concatenate([jnp.where(first_head, x, zero), jnp.where(first_head, zero, x)], axis=0)


def _rwkv_scan_kernel(r_ref, lw_ref, k_ref, v_ref, kk_ref, b_ref, g_ref, bonus_ref, lnw_ref,
                      lnb_ref, y_ref, s_ref, *, chunks):
    @pl.when(pl.program_id(1) == 0)
    def _():
        s_ref[...] = jnp.zeros_like(s_ref)

    c2 = 2 * CHUNK
    tri_incl = (lax.broadcasted_iota(I32, (CHUNK, CHUNK), 1)
                <= lax.broadcasted_iota(I32, (CHUNK, CHUNK), 0)).astype(BF16)
    rowi = lax.broadcasted_iota(I32, (c2, c2), 0)
    coli = lax.broadcasted_iota(I32, (c2, c2), 1)
    strict = coli < rowi
    incl = coli <= rowi
    eye = (coli == rowi).astype(F32)
    first_head = lax.broadcasted_iota(I32, (CHUNK, LANES), 1) < HEAD_DIM
    ones = _group_ones(RWKV_DIM, HEAD_DIM)
    inv_n = 1.0 / HEAD_DIM

    def chunk_body(c, carry):
        rows = pl.ds(pl.multiple_of(c * CHUNK, CHUNK), CHUNK)
        lw = lw_ref[rows, :]
        hi, mid, lo = _split3(lw)
        cl = _dot(tri_incl, hi) + _dot(tri_incl, mid) + _dot(tri_incl, lo)
        cl_end = cl[CHUNK - 1:CHUNK, :]
        r = r_ref[rows, :]
        k = k_ref[rows, :]
        v = v_ref[rows, :]
        kk = kk_ref[rows, :]
        b = b_ref[rows, :]
        inv = jnp.exp(-cl)
        to_end = jnp.exp(cl_end - cl)
        a_t = -kk * jnp.exp(cl - lw)
        r_t = r * jnp.exp(cl)
        b_t = b * inv
        k_t = k * inv
        b_h = b * to_end
        k_h = k * to_end
        p_end = jnp.exp(cl_end)
        ys = []
        for j in range(RWKV_PAIRS):
            ls = slice(j * LANES, (j + 1) * LANES)
            bd = lambda x: _blockdiag(x[:, ls], first_head).astype(BF16)
            a2, r2, b2, k2, v2, bh2, kh2 = (bd(x) for x in (a_t, r_t, b_t, k_t, v, b_h, k_h))
            lhs = jnp.concatenate([a2, r2], axis=0)
            a4 = _dot_nt(lhs, jnp.concatenate([b2, k2], axis=0))
            a_ab = jnp.where(strict, a4[:c2, :c2], 0.0)
            a_ak = jnp.where(strict, a4[:c2, c2:], 0.0)
            a_rb = jnp.where(incl, a4[c2:, :c2], 0.0)
            a_rk = jnp.where(incl, a4[c2:, c2:], 0.0)
            tinv = eye + a_ab
            apow = a_ab
            for _ in range(5):
                ab = apow.astype(BF16)
                apow = _dot(ab, ab)
                tinv = tinv + _dot(tinv.astype(BF16), apow.astype(BF16))
            s2 = s_ref[j]
            u = _dot_nt(lhs, s2.astype(BF16))
            sa = _dot(tinv.astype(BF16), (u[:c2] + _dot(a_ak.astype(BF16), v2)).astype(BF16))
            sav = jnp.concatenate([sa.astype(BF16), v2], axis=0)
            y2 = u[c2:] + _dot(jnp.concatenate([a_rb, a_rk], axis=1).astype(BF16), sav)
            sav_t = jnp.concatenate([sa, v2.astype(F32)], axis=0).T.astype(BF16)
            s_ref[j] = s2 * p_end[:, ls] + _dot(sav_t, jnp.concatenate([bh2, kh2], axis=0))
            ys.append(y2[:CHUNK] + y2[CHUNK:])
        y = jnp.concatenate(ys, axis=1)
        m = _group_sum(y, ones) * inv_n
        d = y - m
        var = _group_sum(d * d, ones) * inv_n
        yn = d * lax.rsqrt(var + LNX_EPS) * lnw_ref[...] + lnb_ref[...]
        y_ref[rows, :] = ((yn + bonus_ref[rows, :]) * g_ref[rows, :]).astype(BF16)
        return carry

    lax.fori_loop(0, chunks, chunk_body, 0)


def _rwkv_scan(r, lw, k, v, kk, b, g, bonus, lnw, lnb, batch, seq, tc):
    n = r.shape[0]
    steps = seq // tc
    row = lambda bi, ti: (bi * steps + ti, 0)
    const = lambda bi, ti: (0, 0)
    blk = pl.BlockSpec((tc, RWKV_DIM), row)
    vec = pl.BlockSpec((1, RWKV_DIM), const)
    return pl.pallas_call(
        functools.partial(_rwkv_scan_kernel, chunks=tc // CHUNK),
        grid=(batch, steps),
        in_specs=[blk] * 8 + [vec, vec],
        out_specs=blk,
        out_shape=jax.ShapeDtypeStruct((n, RWKV_DIM), BF16),
        scratch_shapes=[pltpu.VMEM((RWKV_PAIRS, LANES, LANES), F32)],
        compiler_params=_cparams("parallel", "arbitrary"),
        name="rwkv_scan",
    )(r, lw, k, v, kk, b, g, bonus, lnw, lnb)


def _swa_kernel(sink_ref, q_ref, kvc_ref, kvp_ref, o_ref):
    nblk = pl.program_id(1)
    kv = jnp.concatenate([kvp_ref[...], kvc_ref[...]], axis=0)
    i = lax.broadcasted_iota(I32, (Q_BLOCK, 2 * Q_BLOCK), 0)
    j = lax.broadcasted_iota(I32, (Q_BLOCK, 2 * Q_BLOCK), 1)
    dist = i + Q_BLOCK - j
    valid = (dist >= 0) & (dist < Q_BLOCK) & ((j >= Q_BLOCK) | (nblk > 0))
    distf = dist.astype(F32)
    for h in range(SWA_KV_HEADS):
        kh = kv[:, h * LANES:(h + 1) * LANES]
        vh = kv[:, SWA_KVP + h * LANES:SWA_KVP + (h + 1) * LANES]
        for gi in range(SWA_GROUP):
            hq = h * SWA_GROUP + gi
            cols = slice(hq * LANES, (hq + 1) * LANES)
            slope = 2.0 ** (-ALIBI_MAX * (hq + 1) / SWA_Q_HEADS)
            s = _dot_nt(q_ref[:, cols], kh) - slope * distf
            s = jnp.where(valid, s, NEG_BIG)
            sink = sink_ref[hq]
            m = jnp.maximum(jnp.max(s, axis=-1, keepdims=True), sink)
            e = jnp.exp(s - m)
            den = jnp.sum(e, axis=-1, keepdims=True) + jnp.exp(sink - m)
            o_ref[:, cols] = _dot((e / den).astype(BF16), vh).astype(BF16)


def _swa(p_sw, sinks, batch, seq):
    n = p_sw.shape[0]
    nb = seq // Q_BLOCK
    kv_col = SWA_QP // (2 * SWA_KVP)
    return pl.pallas_call(
        _swa_kernel,
        grid=(batch, nb),
        in_specs=[
            pl.BlockSpec(memory_space=pltpu.SMEM),
            pl.BlockSpec((Q_BLOCK, SWA_QP), lambda bi, ni: (bi * nb + ni, 0)),
            pl.BlockSpec((Q_BLOCK, 2 * SWA_KVP), lambda bi, ni: (bi * nb + ni, kv_col)),
            pl.BlockSpec((Q_BLOCK, 2 * SWA_KVP),
                         lambda bi, ni: (bi * nb + jnp.maximum(ni - 1, 0), kv_col)),
        ],
        out_specs=pl.BlockSpec((Q_BLOCK, SWA_QP), lambda bi, ni: (bi * nb + ni, 0)),
        out_shape=jax.ShapeDtypeStruct((n, SWA_QP), BF16),
        compiler_params=_cparams("parallel", "parallel"),
        name="swa",
    )(sinks, p_sw, p_sw, p_sw)


def _merge_kernel(x_ref, yr_ref, ys_ref, qm_ref, km_ref, vm_ref, anw_ref, wg_ref, wb0_ref, wb1_ref,
                  wb2_ref, wo_ref, fnw_ref, wrh_ref, wrl_ref, br_ref,
                  h_ref, hn_ref, meta_ref, cnt_ref, carry_ref):
    @pl.when(pl.program_id(0) == 0)
    def _():
        carry_ref[...] = jnp.zeros_like(carry_ref)

    x = x_ref[...]
    tm = x.shape[0]
    xb = _rms(x, anw_ref[...]).astype(BF16)
    outs = []
    for hh in range(MEM_HEADS):
        cols = slice(hh * MEM_HEAD_DIM, (hh + 1) * MEM_HEAD_DIM)
        s = _dot_nt(qm_ref[:, cols], km_ref[:, cols]) * (MEM_HEAD_DIM ** -0.5)
        m = jnp.max(s, axis=-1, keepdims=True)
        e = jnp.exp(s - m)
        pr = (e / jnp.sum(e, axis=-1, keepdims=True)).astype(BF16)
        outs.append(_dot(pr, vm_ref[:, cols]))
    y_mem = jnp.concatenate(outs, axis=1).astype(BF16)
    merged = jax.nn.sigmoid(_dot(xb, wg_ref[0])) * _dot(yr_ref[...], wb0_ref[...])
    merged += jax.nn.sigmoid(_dot(xb, wg_ref[1])) * _dot(ys_ref[...], wb1_ref[...])
    merged += jax.nn.sigmoid(_dot(xb, wg_ref[2])) * _dot(y_mem, wb2_ref[...])
    h = x + _dot(merged.astype(BF16), wo_ref[...])
    h_ref[...] = h
    hn = _rms(h, fnw_ref[...])
    hn_ref[...] = hn
    hi, lo = _split2(hn)
    logits = _dot(hi, wrh_ref[...]) + _dot(hi, wrl_ref[...]) + _dot(lo, wrh_ref[...]) + br_ref[...]
    lane = lax.broadcasted_iota(I32, (tm, LANES), 1).astype(F32)
    work = logits
    vals, idxs, sels = [], [], []
    for _ in range(TOP_K):
        mk = jnp.max(work, axis=-1, keepdims=True)
        ik = jnp.min(jnp.where(work == mk, lane, float(LANES)), axis=-1, keepdims=True)
        sel = lane == ik
        work = jnp.where(sel, 2.0 * NEG_BIG, work)
        vals.append(mk)
        idxs.append(ik)
        sels.append(sel)
    es = [jnp.exp(vk - vals[0]) for vk in vals]
    den = es[0] + es[1] + es[2] + es[3]
    cnt = jnp.zeros((tm, LANES), F32)
    for sel in sels:
        cnt = cnt + sel.astype(F32)
    tri = (lax.broadcasted_iota(I32, (tm, tm), 1) < lax.broadcasted_iota(I32, (tm, tm), 0)).astype(BF16)
    prefix = _dot(tri, cnt.astype(BF16)) + carry_ref[0:1, :]
    meta = jnp.zeros((tm, LANES), F32)
    for kk in range(TOP_K):
        pos = jnp.sum(jnp.where(sels[kk], prefix, 0.0), axis=-1, keepdims=True)
        meta = jnp.where(lane == kk, idxs[kk], meta)
        meta = jnp.where(lane == TOP_K + kk, es[kk] / den, meta)
        meta = jnp.where(lane == 2 * TOP_K + kk, pos, meta)
    meta_ref[...] = meta
    total = carry_ref[0:1, :] + jnp.sum(cnt, axis=0, keepdims=True)
    carry_ref[...] = jnp.broadcast_to(total, carry_ref.shape)
    cnt_ref[...] = jnp.broadcast_to(total, cnt_ref.shape)


def _merge(x2, y_rw, y_sw, q_mem, memkv, anw, wg, wb0, wb1, wb2, wo, fnw, wrh, wrl, br, tm, seq,
           mem_len):
    n = x2.shape[0]
    tiles_per_seq = seq // tm
    row = lambda i: (i, 0)
    const = lambda i: (0, 0)
    return pl.pallas_call(
        _merge_kernel,
        grid=(n // tm,),
        in_specs=[
            pl.BlockSpec((tm, D_MODEL), row),
            pl.BlockSpec((tm, RWKV_DIM), row),
            pl.BlockSpec((tm, SWA_QP), row),
            pl.BlockSpec((tm, MEM_DIM), row),
            pl.BlockSpec((mem_len, MEM_DIM), lambda i: (i // tiles_per_seq, 0)),
            pl.BlockSpec((mem_len, MEM_DIM), lambda i: (i // tiles_per_seq, 1)),
            pl.BlockSpec((1, D_MODEL), const),
            pl.BlockSpec((3, D_MODEL, D_MODEL), lambda i: (0, 0, 0)),
            pl.BlockSpec((RWKV_DIM, D_MODEL), const),
            pl.BlockSpec((SWA_QP, D_MODEL), const),
            pl.BlockSpec((MEM_DIM, D_MODEL), const),
            pl.BlockSpec((D_MODEL, D_MODEL), const),
            pl.BlockSpec((1, D_MODEL), const),
            pl.BlockSpec((D_MODEL, LANES), const),
            pl.BlockSpec((D_MODEL, LANES), const),
            pl.BlockSpec((1, LANES), const),
        ],
        out_specs=[
            pl.BlockSpec((tm, D_MODEL), row),
            pl.BlockSpec((tm, D_MODEL), row),
            pl.BlockSpec((tm, LANES), row),
            pl.BlockSpec((8, LANES), const),
        ],
        out_shape=[
            jax.ShapeDtypeStruct((n, D_MODEL), F32),
            jax.ShapeDtypeStruct((n, D_MODEL), F32),
            jax.ShapeDtypeStruct((n, LANES), F32),
            jax.ShapeDtypeStruct((8, LANES), F32),
        ],
        scratch_shapes=[pltpu.VMEM((8, LANES), F32)],
        compiler_params=_cparams("arbitrary"),
        name="merge_router",
    )(x2, y_rw, y_sw, q_mem, memkv, memkv, anw, wg, wb0, wb1, wb2, wo, fnw, wrh, wrl, br)


def _w1_split_kernel(w_ref, g_ref, l_ref, t_ref):
    t_ref[...] = w_ref[...].T
    g_ref[...] = t_ref[pl.ds(0, D_FF, stride=2), :].T.astype(BF16)
    l_ref[...] = t_ref[pl.ds(1, D_FF, stride=2), :].T.astype(BF16)


def _w1_split(w1, rows):
    n_e = w1.shape[0]
    out = pl.BlockSpec((None, rows, D_FF), lambda e, r: (e, r, 0))
    return pl.pallas_call(
        _w1_split_kernel,
        grid=(n_e, D_MODEL // rows),
        in_specs=[pl.BlockSpec((None, rows, 2 * D_FF), lambda e, r: (e, r, 0))],
        out_specs=[out, out],
        out_shape=[jax.ShapeDtypeStruct((n_e, D_MODEL, D_FF), BF16)] * 2,
        scratch_shapes=[pltpu.VMEM((2 * D_FF, rows), F32)],
        compiler_params=_cparams("parallel", "parallel"),
        name="w1_split",
    )(w1)


def _row_copy(src_hbm, dst_hbm, idx_smem, xbuf, ybuf, sem, j, gather):
    if gather:
        return pltpu.make_async_copy(src_hbm.at[pl.ds(idx_smem[j], 1), :],
                                     xbuf.at[pl.ds(j, 1), :], sem.at[0])
    return pltpu.make_async_copy(ybuf.at[pl.ds(j, 1), :],
                                 dst_hbm.at[pl.ds(idx_smem[EXPERT_BLOCK + j], 1), :], sem.at[1])


def _moe_kernel(be_ref, nused_ref, idx_hbm, hn_hbm, w1g_ref, w1l_ref, b1g_ref, b1l_ref, w2_ref,
                b2_ref, y_hbm, idx_smem, xbuf, ybuf, sem, isem):
    i = pl.program_id(0)

    @pl.when(i == 0)
    def _():
        ybuf[...] = jnp.zeros_like(ybuf)
        dump = pltpu.make_async_copy(
            ybuf, y_hbm.at[pl.ds(y_hbm.shape[0] - EXPERT_BLOCK, EXPERT_BLOCK), :], sem.at[1])
        dump.start()
        dump.wait()

    @pl.when(i < nused_ref[0])
    def _():
        icp =pltpu.make_async_copy(idx_hbm.at[i], idx_smem, isem.at[0])
        icp.start()
        icp.wait()
        for j in range(EXPERT_BLOCK):
            _row_copy(hn_hbm, y_hbm, idx_smem, xbuf, ybuf, sem, j, True).start()
        for j in range(EXPERT_BLOCK):
            _row_copy(hn_hbm, y_hbm, idx_smem, xbuf, ybuf, sem, j, True).wait()
        xb = xbuf[...].astype(BF16)
        glu = _dot(xb, w1g_ref[...]) + b1g_ref[...]
        lin = _dot(xb, w1l_ref[...]) + b1l_ref[...]
        glu = jnp.minimum(glu, SWIGLU_LIMIT)
        lin = jnp.clip(lin, -SWIGLU_LIMIT, SWIGLU_LIMIT)
        act = glu * jax.nn.sigmoid(SWIGLU_ALPHA * glu) * (lin + 1.0)
        ybuf[...] = _dot(act.astype(BF16), w2_ref[...]) + b2_ref[...]
        for j in range(EXPERT_BLOCK):
            _row_copy(hn_hbm, y_hbm, idx_smem, xbuf, ybuf, sem, j, False).start()
        for j in range(EXPERT_BLOCK):
            _row_copy(hn_hbm, y_hbm, idx_smem, xbuf, ybuf, sem, j, False).wait()


def _moe_ffn(block_e, n_used, slot_idx, hn, w1g, w1l, b1g, b1l, w2, b2, n_rows_out):
    n_blocks = slot_idx.shape[0]
    wspec = pl.BlockSpec((None, D_MODEL, D_FF), lambda i, be, nu: (be[i], 0, 0))
    bspec = pl.BlockSpec((None, 1, D_FF), lambda i, be, nu: (be[i], 0, 0))
    grid_spec = pltpu.PrefetchScalarGridSpec(
        num_scalar_prefetch=2,
        grid=(n_blocks,),
        in_specs=[
            pl.BlockSpec(memory_space=pl.ANY),
            pl.BlockSpec(memory_space=pl.ANY),
            wspec, wspec, bspec, bspec,
            pl.BlockSpec((None, D_FF, D_MODEL), lambda i, be, nu: (be[i], 0, 0)),
            pl.BlockSpec((None, 1, D_MODEL), lambda i, be, nu: (be[i], 0, 0)),
        ],
        out_specs=pl.BlockSpec(memory_space=pl.ANY),
        scratch_shapes=[
            pltpu.SMEM((2 * EXPERT_BLOCK,), I32),
            pltpu.VMEM((EXPERT_BLOCK, D_MODEL), F32),
            pltpu.VMEM((EXPERT_BLOCK, D_MODEL), F32),
            pltpu.SemaphoreType.DMA((2,)),
            pltpu.SemaphoreType.DMA((1,)),
        ],
    )
    return pl.pallas_call(
        _moe_kernel,
        grid_spec=grid_spec,
        out_shape=jax.ShapeDtypeStruct((n_rows_out, D_MODEL), F32),
        compiler_params=_cparams("arbitrary"),
        name="moe_ffn",
    )(block_e, n_used, slot_idx, hn, w1g, w1l, b1g, b1l, w2, b2)


def _combine_kernel(h_ref, y0_ref, y1_ref, y2_ref, y3_ref, meta_ref, fw_ref, o_ref):
    acc = h_ref[...]
    meta = meta_ref[...]
    for kk, y_ref in enumerate((y0_ref, y1_ref, y2_ref, y3_ref)):
        acc = acc + meta[:, TOP_K + kk:TOP_K + kk + 1] * y_ref[...]
    o_ref[...] = _rms(acc, fw_ref[...])


def _combine(h, y_flat, meta, fw, tm):
    n = h.shape[0]
    tiles = n // tm
    row = lambda i: (i, 0)
    y_specs = [pl.BlockSpec((tm, D_MODEL), functools.partial(lambda i, kk: (kk * tiles + i, 0), kk=kk))
               for kk in range(TOP_K)]
    return pl.pallas_call(
        _combine_kernel,
        grid=(tiles,),
        in_specs=[pl.BlockSpec((tm, D_MODEL), row)] + y_specs + [
            pl.BlockSpec((tm, LANES), row),
            pl.BlockSpec((1, D_MODEL), lambda i: (0, 0)),
        ],
        out_specs=pl.BlockSpec((tm, D_MODEL), row),
        out_shape=jax.ShapeDtypeStruct((n, D_MODEL), F32),
        compiler_params=_cparams("parallel"),
        name="combine",
    )(h, y_flat, y_flat, y_flat, y_flat, meta, fw)


def _pad_heads_cols(w, heads):
    lead = w.shape[:-1]
    w = w.reshape(lead + (heads, HEAD_DIM))
    w = jnp.pad(w, [(0, 0)] * len(lead) + [(0, 0), (0, LANES - HEAD_DIM)])
    return w.reshape(lead + (heads * LANES,))


def _pick_tile(n, pref):
    t = pref
    while n % t:
        t //= 2
    return t


def kernel(x, mem, attn_norm_w, mem_norm_w, w_in, b_swa, rwkv_mu, rwkv_w0, rwkv_w2, rwkv_a0,
           rwkv_a2, rwkv_g2, rwkv_kk, rwkv_ka, rwkv_rk, rwkv_lnx_w, rwkv_lnx_b, swa_sinks,
           w_mem_kv, w_gate, w_branch, w_out, ffn_norm_w, w_router, b_router, w_exp1, b_exp1,
           w_exp2, b_exp2, final_norm_w):
    batch, seq, d = x.shape
    mem_len = mem.shape[1]
    n = batch * seq
    layer = 0
    x2 = x.reshape(n, d)
    row = lambda a: a.reshape(1, -1)

    w_in_l = w_in[layer]
    w_rw = w_in_l[:, :RWKV_PROJ].astype(BF16)
    sw = w_in_l[:, RWKV_PROJ:RWKV_PROJ + SWA_DIM + 2 * SWA_KV_DIM]
    bs = b_swa[layer]
    scale = HEAD_DIM ** -0.5
    w_sw = jnp.concatenate([
        _pad_heads_cols(sw[:, :SWA_DIM] * scale, SWA_Q_HEADS),
        _pad_heads_cols(sw[:, SWA_DIM:SWA_DIM + SWA_KV_DIM], SWA_KV_HEADS),
        _pad_heads_cols(sw[:, SWA_DIM + SWA_KV_DIM:], SWA_KV_HEADS)], axis=1).astype(BF16)
    b_sw = row(jnp.concatenate([
        _pad_heads_cols(bs[:SWA_DIM] * scale, SWA_Q_HEADS),
        _pad_heads_cols(bs[SWA_DIM:SWA_DIM + SWA_KV_DIM], SWA_KV_HEADS),
        _pad_heads_cols(bs[SWA_DIM + SWA_KV_DIM:], SWA_KV_HEADS)]))
    w_qm = w_in_l[:, RWKV_PROJ + SWA_DIM + 2 * SWA_KV_DIM:].astype(BF16)
    zeros_l = jnp.zeros((DECAY_LORA, RWKV_DIM), F32)
    wwa = jnp.concatenate([
        jnp.concatenate([rwkv_w2[layer], zeros_l], axis=1),
        jnp.concatenate([zeros_l, rwkv_a2[layer]], axis=1)], axis=0).astype(BF16)
    wb = w_branch[layer]
    wb1 = _pad_heads_cols(wb[1].T, SWA_Q_HEADS).T.astype(BF16)
    wr = jnp.pad(w_router[layer], ((0, 0), (0, LANES - N_EXPERTS)))
    wrh = wr.astype(BF16)
    wrl = (wr - wrh.astype(F32)).astype(BF16)
    br = row(jnp.pad(b_router[layer], (0, LANES - N_EXPERTS), constant_values=NEG_BIG))
    w1 = w_exp1[layer]
    b1 = b_exp1[layer]
    w1g, w1l = _w1_split(w1, LANES)
    b1g = b1[:, None, 0::2]
    b1l = b1[:, None, 1::2]
    w2 = w_exp2[layer].astype(BF16)
    b2 = b_exp2[layer][:, None, :]

    tm_proj = _pick_tile(seq, 512)
    p_rw, p_sw, q_mem = _inproj(x2, row(attn_norm_w[layer]), w_rw, w_sw, w_qm, b_sw, tm_proj)
    memkv = _memkv(mem.reshape(batch * mem_len, d), row(mem_norm_w[layer]),
                   w_mem_kv[layer].astype(BF16), mem_len)
    r, lw, k, v, kk, b, g, bonus = _rwkv_prep(
        p_rw, row(rwkv_mu[layer]), row(rwkv_w0[layer]), wwa, row(rwkv_a0[layer]),
        rwkv_g2[layer].astype(BF16), row(rwkv_kk[layer]), row(rwkv_ka[layer]),
        row(rwkv_rk[layer]), tm_proj, seq)
    y_rw = _rwkv_scan(r, lw, k, v, kk, b, g, bonus, row(rwkv_lnx_w[layer]),
                      row(rwkv_lnx_b[layer]), batch, seq, _pick_tile(seq, 256))
    y_sw = _swa(p_sw, swa_sinks[layer], batch, seq)

    tm_merge = _pick_tile(seq, 256)
    h, hn, meta, counts = _merge(
        x2, y_rw, y_sw, q_mem, memkv, row(attn_norm_w[layer]), w_gate[layer].astype(BF16),
        wb[0].astype(BF16), wb1, wb[2].astype(BF16), w_out[layer].astype(BF16),
        row(ffn_norm_w[layer]), wrh, wrl, br, tm_merge, seq, mem_len)

    a_total = n * TOP_K
    e_idx = meta[:, 0:TOP_K].astype(I32)
    pos = meta[:, 2 * TOP_K:3 * TOP_K].astype(I32)
    cnt = counts[0, :N_EXPERTS].astype(I32)
    padded = ((cnt + EXPERT_BLOCK - 1) // EXPERT_BLOCK) * EXPERT_BLOCK
    pad_ends = jnp.cumsum(padded)
    pad_starts = pad_ends - padded
    n_blocks = -(-a_total // EXPERT_BLOCK) + N_EXPERTS
    p_slots = n_blocks * EXPERT_BLOCK
    dest = (pad_starts[e_idx] + pos).reshape(-1)
    a_ids = jnp.arange(a_total, dtype=I32)
    slot_a = jnp.full((p_slots,), -1, I32).at[dest].set(a_ids)
    lane_id = jnp.arange(p_slots, dtype=I32) % EXPERT_BLOCK
    slot_tok = jnp.where(slot_a >= 0, slot_a // TOP_K, 0)
    slot_dst = jnp.where(slot_a >= 0, (slot_a % TOP_K) * n + slot_a // TOP_K, a_total + lane_id)
    slot_idx = jnp.concatenate([slot_tok.reshape(n_blocks, EXPERT_BLOCK),
                                slot_dst.reshape(n_blocks, EXPERT_BLOCK)], axis=1)
    block_start = jnp.arange(n_blocks, dtype=I32) * EXPERT_BLOCK
    block_e = jnp.minimum(jnp.sum(pad_ends[None, :] <= block_start[:, None], axis=1),
                          N_EXPERTS - 1).astype(I32)
    n_used = (pad_ends[-1:] // EXPERT_BLOCK).astype(I32)

    y_flat = _moe_ffn(block_e, n_used, slot_idx, hn, w1g, w1l, b1g, b1l, w2, b2,
                      a_total + EXPERT_BLOCK)
    out = _combine(h, y_flat, meta, row(final_norm_w), tm_merge)
    return out.reshape(batch, seq, d)
```

```python
import functools

import jax
import jax.numpy as jnp
from jax import lax
from jax.experimental import pallas as pl
from jax.experimental.pallas import tpu as pltpu

F32 = jnp.float32
BF16 = jnp.bfloat16
I32 = jnp.int32

D_MODEL = 1024
HEAD_DIM = 64
LANES = 128
RWKV_DIM = 512
RWKV_PAIRS = RWKV_DIM // LANES
DECAY_LORA = 64
AAA_LORA = 64
GATE_LORA = 128
RWKV_PROJ = 3 * RWKV_DIM + DECAY_LORA + AAA_LORA + GATE_LORA
LNX_EPS = 64e-5
SWA_Q_HEADS = 8
SWA_KV_HEADS = 2
SWA_GROUP = SWA_Q_HEADS // SWA_KV_HEADS
SWA_DIM = SWA_Q_HEADS * HEAD_DIM
SWA_KV_DIM = SWA_KV_HEADS * HEAD_DIM
SWA_QP = SWA_Q_HEADS * LANES
SWA_KVP = SWA_KV_HEADS * LANES
SWA_PROJ_P = SWA_QP + 2 * SWA_KVP
Q_BLOCK = 128
ALIBI_MAX = 8.0
MEM_HEADS = 4
MEM_HEAD_DIM = 128
MEM_DIM = MEM_HEADS * MEM_HEAD_DIM
N_EXPERTS = 32
TOP_K = 4
D_FF = 1024
SWIGLU_ALPHA = 1.702
SWIGLU_LIMIT = 7.0
EXPERT_BLOCK = 128
NORM_EPS = 1e-5
CHUNK = 64
NEG_BIG = -1e30

VMEM_LIMIT = 56 * 1024 * 1024


def _cparams(*sem):
    return pltpu.CompilerParams(dimension_semantics=sem, vmem_limit_bytes=VMEM_LIMIT)


def _rms(x, w):
    return x * lax.rsqrt(jnp.mean(x * x, axis=-1, keepdims=True) + NORM_EPS) * w


def _dot(a, b):
    return jnp.dot(a, b, preferred_element_type=F32)


def _dot_nt(a, b):
    return lax.dot_general(a, b, (((1,), (1,)), ((), ())), preferred_element_type=F32)


def _split2(x):
    hi = x.astype(BF16)
    lo = (x - hi.astype(F32)).astype(BF16)
    return hi, lo


def _split3(x):
    hi = x.astype(BF16)
    r1 = x - hi.astype(F32)
    mid = r1.astype(BF16)
    lo = (r1 - mid.astype(F32)).astype(BF16)
    return hi, mid, lo


ROW_TILES = D_MODEL // LANES


def _store_row_tiles(ref, x):
    rows = x.shape[0]
    for c in range(ROW_TILES):
        ref[pl.ds(c, rows, stride=ROW_TILES), :] = x[:, c * LANES:(c + 1) * LANES]


def _load_row_tiles(ref, rows):
    return jnp.concatenate(
        [ref[pl.ds(c, rows, stride=ROW_TILES), :] for c in range(ROW_TILES)], axis=1)


def _group_ones(n, group):
    r = lax.broadcasted_iota(I32, (n, n), 0) // group
    c = lax.broadcasted_iota(I32, (n, n), 1) // group
    return (r == c).astype(BF16)


def _group_sum(x, ones):
    hi, lo = _split2(x)
    return _dot(hi, ones) + _dot(lo, ones)


def _inproj_kernel(x_ref, nw_ref, wr_ref, ws_ref, wq_ref, bs_ref, pr_ref, ps_ref, qm_ref):
    xb = _rms(x_ref[...], nw_ref[...]).astype(BF16)
    pr_ref[...] = _dot(xb, wr_ref[...])
    ps_ref[...] = (_dot(xb, ws_ref[...]) + bs_ref[...]).astype(BF16)
    qm_ref[...] = _dot(xb, wq_ref[...]).astype(BF16)


def _inproj(x2, nw, w_rw, w_sw, w_qm, b_sw, tm):
    n = x2.shape[0]
    const = lambda i: (0, 0)
    row = lambda i: (i, 0)
    return pl.pallas_call(
        _inproj_kernel,
        grid=(n // tm,),
        in_specs=[
            pl.BlockSpec((tm, D_MODEL), row),
            pl.BlockSpec((1, D_MODEL), const),
            pl.BlockSpec((D_MODEL, RWKV_PROJ), const),
            pl.BlockSpec((D_MODEL, SWA_PROJ_P), const),
            pl.BlockSpec((D_MODEL, MEM_DIM), const),
            pl.BlockSpec((1, SWA_PROJ_P), const),
        ],
        out_specs=[
            pl.BlockSpec((tm, RWKV_PROJ), row),
            pl.BlockSpec((tm, SWA_PROJ_P), row),
            pl.BlockSpec((tm, MEM_DIM), row),
        ],
        out_shape=[
            jax.ShapeDtypeStruct((n, RWKV_PROJ), F32),
            jax.ShapeDtypeStruct((n, SWA_PROJ_P), BF16),
            jax.ShapeDtypeStruct((n, MEM_DIM), BF16),
        ],
        compiler_params=_cparams("parallel"),
        name="inproj",
    )(x2, nw, w_rw, w_sw, w_qm, b_sw)


def _memkv_kernel(m_ref, nw_ref, w_ref, o_ref):
    mb = _rms(m_ref[...], nw_ref[...]).astype(BF16)
    o_ref[...] = _dot(mb, w_ref[...]).astype(BF16)


def _memkv(mem2, nw, w, tm):
    n = mem2.shape[0]
    return pl.pallas_call(
        _memkv_kernel,
        grid=(n // tm,),
        in_specs=[
            pl.BlockSpec((tm, D_MODEL), lambda i: (i, 0)),
            pl.BlockSpec((1, D_MODEL), lambda i: (0, 0)),
            pl.BlockSpec((D_MODEL, 2 * MEM_DIM), lambda i: (0, 0)),
        ],
        out_specs=pl.BlockSpec((tm, 2 * MEM_DIM), lambda i: (i, 0)),
        out_shape=jax.ShapeDtypeStruct((n, 2 * MEM_DIM), BF16),
        compiler_params=_cparams("parallel"),
        name="memkv",
    )(mem2, nw, w)


def _rwkv_prep_kernel(p_ref, halo_ref, mu_ref, w0_ref, wwa_ref, a0_ref, g2_ref, kk_ref, ka_ref,
                      rk_ref, r_out, lw_out, k_out, v_out, kk_out, b_out, g_out, bonus_out,
                      *, tiles_per_seq):
    i = pl.program_id(0)
    p = p_ref[...]
    tm = p.shape[0]
    halo = jnp.where(i % tiles_per_seq == 0, 0.0, halo_ref[7:8, :])
    row = lax.broadcasted_iota(I32, p.shape, 0)
    prev = jnp.where(row == 0, halo, pltpu.roll(p, 1, axis=0))
    ps = p + (prev - p) * mu_ref[...]
    r = ps[:, 0:RWKV_DIM]
    k = ps[:, RWKV_DIM:2 * RWKV_DIM]
    v = ps[:, 2 * RWKV_DIM:3 * RWKV_DIM]
    wa = ps[:, 3 * RWKV_DIM:3 * RWKV_DIM + LANES]
    pg = ps[:, 3 * RWKV_DIM + LANES:]
    lane = lax.broadcasted_iota(I32, (tm, LANES), 1)
    z = jnp.where(lane < DECAY_LORA, jnp.tanh(wa), wa).astype(BF16)
    lora = _dot(z, wwa_ref[...])
    w_log = -jax.nn.softplus(-(w0_ref[...] + lora[:, :RWKV_DIM])) - 0.5
    lw_out[...] = -jnp.exp(w_log)
    alr = jax.nn.sigmoid(a0_ref[...] + lora[:, RWKV_DIM:])
    g_out[...] = _dot(jax.nn.sigmoid(pg).astype(BF16), g2_ref[...])
    ones = _group_ones(RWKV_DIM, HEAD_DIM)
    kk = k * kk_ref[...]
    kk = kk / jnp.maximum(jnp.sqrt(_group_sum(kk * kk, ones)), 1e-12)
    k2 = k * (1.0 + (alr - 1.0) * ka_ref[...])
    r_out[...] = r
    k_out[...] = k2
    v_out[...] = v
    kk_out[...] = kk
    b_out[...] = kk * alr
    bonus_out[...] = _group_sum(r * k2 * rk_ref[...], ones) * v


def _rwkv_prep(p_rw, mu, w0, wwa, a0, g2, k_k, k_a, r_k, tm, seq):
    n = p_rw.shape[0]
    tiles_per_seq = seq // tm
    const = lambda i: (0, 0)
    row = lambda i: (i, 0)
    vec = pl.BlockSpec((1, RWKV_DIM), const)
    out = pl.BlockSpec((tm, RWKV_DIM), row)
    return pl.pallas_call(
        functools.partial(_rwkv_prep_kernel, tiles_per_seq=tiles_per_seq),
        grid=(n // tm,),
        in_specs=[
            pl.BlockSpec((tm, RWKV_PROJ), row),
            pl.BlockSpec((8, RWKV_PROJ), lambda i: (jnp.maximum(i * (tm // 8) - 1, 0), 0)),
            pl.BlockSpec((1, RWKV_PROJ), const),
            vec,
            pl.BlockSpec((LANES, 2 * RWKV_DIM), const),
            vec,
            pl.BlockSpec((GATE_LORA, RWKV_DIM), const),
            vec, vec, vec,
        ],
        out_specs=[out] * 8,
        out_shape=[jax.ShapeDtypeStruct((n, RWKV_DIM), F32)] * 8,
        compiler_params=_cparams("parallel"),
        name="rwkv_prep",
    )(p_rw, p_rw, mu, w0, wwa, a0, g2, k_k, k_a, r_k)


def _blockdiag(x, first_head):
    zero = jnp.zeros_like(x)
    return jnp.concatenate([jnp.where(first_head, x, zero), jnp.where(first_head, zero, x)], axis=0)


def _rwkv_scan_kernel(r_ref, lw_ref, k_ref, v_ref, kk_ref, b_ref, g_ref, bonus_ref, lnw_ref,
                      lnb_ref, y_ref, s_ref, *, chunks):
    @pl.when(pl.program_id(1) == 0)
    def _():
        s_ref[...] = jnp.zeros_like(s_ref)

    c2 = 2 * CHUNK
    tri_incl = (lax.broadcasted_iota(I32, (CHUNK, CHUNK), 1)
                <= lax.broadcasted_iota(I32, (CHUNK, CHUNK), 0)).astype(BF16)
    rowi = lax.broadcasted_iota(I32, (c2, c2), 0)
    coli = lax.broadcasted_iota(I32, (c2, c2), 1)
    strict = coli < rowi
    incl = coli <= rowi
    eye = (coli == rowi).astype(F32)
    first_head = lax.broadcasted_iota(I32, (CHUNK, LANES), 1) < HEAD_DIM
    ones = _group_ones(RWKV_DIM, HEAD_DIM)
    inv_n = 1.0 / HEAD_DIM

    def chunk_body(c, carry):
        rows = pl.ds(pl.multiple_of(c * CHUNK, CHUNK), CHUNK)
        lw = lw_ref[rows, :]
        hi, mid, lo = _split3(lw)
        cl = _dot(tri_incl, hi) + _dot(tri_incl, mid) + _dot(tri_incl, lo)
        cl_end = cl[CHUNK - 1:CHUNK, :]
        r = r_ref[rows, :]
        k = k_ref[rows, :]
        v = v_ref[rows, :]
        kk = kk_ref[rows, :]
        b = b_ref[rows, :]
        inv = jnp.exp(-cl)
        to_end = jnp.exp(cl_end - cl)
        a_t = -kk * jnp.exp(cl - lw)
        r_t = r * jnp.exp(cl)
        b_t = b * inv
        k_t = k * inv
        b_h = b * to_end
        k_h = k * to_end
        p_end = jnp.exp(cl_end)
        pairs = range(RWKV_PAIRS)
        lanes = [slice(j * LANES, (j + 1) * LANES) for j in pairs]
        bd = lambda x, j: _blockdiag(x[:, lanes[j]], first_head).astype(BF16)
        lhs = [jnp.concatenate([bd(a_t, j), bd(r_t, j)], axis=0) for j in pairs]
        a4 = [_dot_nt(lhs[j], jnp.concatenate([bd(b_t, j), bd(k_t, j)], axis=0)) for j in pairs]
        a_ab = [jnp.where(strict, a4[j][:c2, :c2], 0.0) for j in pairs]
        tinv = [eye + a_ab[j] for j in pairs]
        ab = [a_ab[j].astype(BF16) for j in pairs]
        apow = [_dot(ab[j], ab[j]) for j in pairs]
        for _ in range(4):
            st = [_dot(jnp.concatenate([tinv[j], apow[j]], axis=0).astype(BF16),
                       apow[j].astype(BF16)) for j in pairs]
            tinv = [tinv[j] + st[j][:c2] for j in pairs]
            apow = [st[j][c2:] for j in pairs]
        tinv = [(tinv[j] + _dot(tinv[j].astype(BF16), apow[j].astype(BF16))).astype(BF16)
                for j in pairs]
        v2 = [bd(v, j) for j in pairs]
        a_ak = [jnp.where(strict, a4[j][:c2, c2:], 0.0).astype(BF16) for j in pairs]
        a_r = [jnp.concatenate([jnp.where(incl, a4[j][c2:, :c2], 0.0),
                                jnp.where(incl, a4[j][c2:, c2:], 0.0)], axis=1).astype(BF16)
               for j in pairs]
        akv = [_dot(a_ak[j], v2[j]) for j in pairs]
        s_old = [s_ref[j] for j in pairs]
        u = [_dot_nt(lhs[j], s_old[j].astype(BF16)) for j in pairs]
        sa = [_dot(tinv[j], (u[j][:c2] + akv[j]).astype(BF16)) for j in pairs]
        y2 = [u[j][c2:] + _dot(a_r[j], jnp.concatenate([sa[j].astype(BF16), v2[j]], axis=0))
              for j in pairs]
        sav_t = [jnp.concatenate([sa[j], v2[j].astype(F32)], axis=0).T.astype(BF16) for j in pairs]
        for j in pairs:
            s_ref[j] = s_old[j] * p_end[:, lanes[j]] + _dot(
                sav_t[j], jnp.concatenate([bd(b_h, j), bd(k_h, j)], axis=0))
        y = jnp.concatenate([y2[j][:CHUNK] + y2[j][CHUNK:] for j in pairs], axis=1)
        m = _group_sum(y, ones) * inv_n
        d = y - m
        var = _group_sum(d * d, ones) * inv_n
        yn = d * lax.rsqrt(var + LNX_EPS) * lnw_ref[...] + lnb_ref[...]
        y_ref[rows, :] = ((yn + bonus_ref[rows, :]) * g_ref[rows, :]).astype(BF16)
        return carry

    lax.fori_loop(0, chunks, chunk_body, 0)


def _rwkv_scan(r, lw, k, v, kk, b, g, bonus, lnw, lnb, batch, seq, tc):
    n = r.shape[0]
    steps = seq // tc
    row = lambda bi, ti: (bi * steps + ti, 0)
    const = lambda bi, ti: (0, 0)
    blk = pl.BlockSpec((tc, RWKV_DIM), row)
    vec = pl.BlockSpec((1, RWKV_DIM), const)
    return pl.pallas_call(
        functools.partial(_rwkv_scan_kernel, chunks=tc // CHUNK),
        grid=(batch, steps),
        in_specs=[blk] * 8 + [vec, vec],
        out_specs=blk,
        out_shape=jax.ShapeDtypeStruct((n, RWKV_DIM), BF16),
        scratch_shapes=[pltpu.VMEM((RWKV_PAIRS, LANES, LANES), F32)],
        compiler_params=_cparams("parallel", "arbitrary"),
        name="rwkv_scan",
    )(r, lw, k, v, kk, b, g, bonus, lnw, lnb)


def _swa_kernel(sink_ref, q_ref, kvc_ref, kvp_ref, o_ref):
    nblk = pl.program_id(1)
    kv = jnp.concatenate([kvp_ref[...], kvc_ref[...]], axis=0)
    i = lax.broadcasted_iota(I32, (Q_BLOCK, 2 * Q_BLOCK), 0)
    j = lax.broadcasted_iota(I32, (Q_BLOCK, 2 * Q_BLOCK), 1)
    dist = i + Q_BLOCK - j
    valid = (dist >= 0) & (dist < Q_BLOCK) & ((j >= Q_BLOCK) | (nblk > 0))
    distf = dist.astype(F32)
    for h in range(SWA_KV_HEADS):
        kh = kv[:, h * LANES:(h + 1) * LANES]
        vh = kv[:, SWA_KVP + h * LANES:SWA_KVP + (h + 1) * LANES]
        for gi in range(SWA_GROUP):
            hq = h * SWA_GROUP + gi
            cols = slice(hq * LANES, (hq + 1) * LANES)
            slope = 2.0 ** (-ALIBI_MAX * (hq + 1) / SWA_Q_HEADS)
            s = _dot_nt(q_ref[:, cols], kh) - slope * distf
            s = jnp.where(valid, s, NEG_BIG)
            sink = sink_ref[hq]
            m = jnp.maximum(jnp.max(s, axis=-1, keepdims=True), sink)
            e = jnp.exp(s - m)
            den = jnp.sum(e, axis=-1, keepdims=True) + jnp.exp(sink - m)
            o_ref[:, cols] = _dot((e / den).astype(BF16), vh).astype(BF16)


def _swa(p_sw, sinks, batch, seq):
    n = p_sw.shape[0]
    nb = seq // Q_BLOCK
    kv_col = SWA_QP // (2 * SWA_KVP)
    return pl.pallas_call(
        _swa_kernel,
        grid=(batch, nb),
        in_specs=[
            pl.BlockSpec(memory_space=pltpu.SMEM),
            pl.BlockSpec((Q_BLOCK, SWA_QP), lambda bi, ni: (bi * nb + ni, 0)),
            pl.BlockSpec((Q_BLOCK, 2 * SWA_KVP), lambda bi, ni: (bi * nb + ni, kv_col)),
            pl.BlockSpec((Q_BLOCK, 2 * SWA_KVP),
                         lambda bi, ni: (bi * nb + jnp.maximum(ni - 1, 0), kv_col)),
        ],
        out_specs=pl.BlockSpec((Q_BLOCK, SWA_QP), lambda bi, ni: (bi * nb + ni, 0)),
        out_shape=jax.ShapeDtypeStruct((n, SWA_QP), BF16),
        compiler_params=_cparams("parallel", "parallel"),
        name="swa",
    )(sinks, p_sw, p_sw, p_sw)


def _merge_kernel(x_ref, yr_ref, ys_ref, qm_ref, km_ref, vm_ref, anw_ref, wg_ref, wb0_ref, wb1_ref,
                  wb2_ref, wo_ref, fnw_ref, wrh_ref, wrl_ref, br_ref,
                  h_ref, hn_ref, meta_ref, cnt_ref, carry_ref):
    @pl.when(pl.program_id(0) == 0)
    def _():
        carry_ref[...] = jnp.zeros_like(carry_ref)

    x = x_ref[...]
    tm = x.shape[0]
    xb = _rms(x, anw_ref[...]).astype(BF16)
    outs = []
    for hh in range(MEM_HEADS):
        cols = slice(hh * MEM_HEAD_DIM, (hh + 1) * MEM_HEAD_DIM)
        s = _dot_nt(qm_ref[:, cols], km_ref[:, cols]) * (MEM_HEAD_DIM ** -0.5)
        m = jnp.max(s, axis=-1, keepdims=True)
        e = jnp.exp(s - m)
        pr = (e / jnp.sum(e, axis=-1, keepdims=True)).astype(BF16)
        outs.append(_dot(pr, vm_ref[:, cols]))
    y_mem = jnp.concatenate(outs, axis=1).astype(BF16)
    merged = jax.nn.sigmoid(_dot(xb, wg_ref[0])) * _dot(yr_ref[...], wb0_ref[...])
    merged += jax.nn.sigmoid(_dot(xb, wg_ref[1])) * _dot(ys_ref[...], wb1_ref[...])
    merged += jax.nn.sigmoid(_dot(xb, wg_ref[2])) * _dot(y_mem, wb2_ref[...])
    h = x + _dot(merged.astype(BF16), wo_ref[...])
    h_ref[...] = h
    hn = _rms(h, fnw_ref[...])
    _store_row_tiles(hn_ref, hn)
    hi, lo = _split2(hn)
    logits = _dot(hi, wrh_ref[...]) + _dot(hi, wrl_ref[...]) + _dot(lo, wrh_ref[...]) + br_ref[...]
    lane = lax.broadcasted_iota(I32, (tm, LANES), 1).astype(F32)
    work = logits
    vals, idxs, sels = [], [], []
    for _ in range(TOP_K):
        mk = jnp.max(work, axis=-1, keepdims=True)
        ik = jnp.min(jnp.where(work == mk, lane, float(LANES)), axis=-1, keepdims=True)
        sel = lane == ik
        work = jnp.where(sel, 2.0 * NEG_BIG, work)
        vals.append(mk)
        idxs.append(ik)
        sels.append(sel)
    es = [jnp.exp(vk - vals[0]) for vk in vals]
    den = es[0] + es[1] + es[2] + es[3]
    cnt = jnp.zeros((tm, LANES), F32)
    for sel in sels:
        cnt = cnt + sel.astype(F32)
    tri = (lax.broadcasted_iota(I32, (tm, tm), 1) < lax.broadcasted_iota(I32, (tm, tm), 0)).astype(BF16)
    prefix = _dot(tri, cnt.astype(BF16)) + carry_ref[0:1, :]
    meta = jnp.zeros((tm, LANES), F32)
    for kk in range(TOP_K):
        pos = jnp.sum(jnp.where(sels[kk], prefix, 0.0), axis=-1, keepdims=True)
        meta = jnp.where(lane == kk, idxs[kk], meta)
        meta = jnp.where(lane == TOP_K + kk, es[kk] / den, meta)
        meta = jnp.where(lane == 2 * TOP_K + kk, pos, meta)
    meta_ref[...] = meta
    total = carry_ref[0:1, :] + jnp.sum(cnt, axis=0, keepdims=True)
    carry_ref[...] = jnp.broadcast_to(total, carry_ref.shape)
    cnt_ref[...] = jnp.broadcast_to(total, cnt_ref.shape)


def _merge(x2, y_rw, y_sw, q_mem, memkv, anw, wg, wb0, wb1, wb2, wo, fnw, wrh, wrl, br, tm, seq,
           mem_len):
    n = x2.shape[0]
    tiles_per_seq = seq // tm
    row = lambda i: (i, 0)
    const = lambda i: (0, 0)
    return pl.pallas_call(
        _merge_kernel,
        grid=(n // tm,),
        in_specs=[
            pl.BlockSpec((tm, D_MODEL), row),
            pl.BlockSpec((tm, RWKV_DIM), row),
            pl.BlockSpec((tm, SWA_QP), row),
            pl.BlockSpec((tm, MEM_DIM), row),
            pl.BlockSpec((mem_len, MEM_DIM), lambda i: (i // tiles_per_seq, 0)),
            pl.BlockSpec((mem_len, MEM_DIM), lambda i: (i // tiles_per_seq, 1)),
            pl.BlockSpec((1, D_MODEL), const),
            pl.BlockSpec((3, D_MODEL, D_MODEL), lambda i: (0, 0, 0)),
            pl.BlockSpec((RWKV_DIM, D_MODEL), const),
            pl.BlockSpec((SWA_QP, D_MODEL), const),
            pl.BlockSpec((MEM_DIM, D_MODEL), const),
            pl.BlockSpec((D_MODEL, D_MODEL), const),
            pl.BlockSpec((1, D_MODEL), const),
            pl.BlockSpec((D_MODEL, LANES), const),
            pl.BlockSpec((D_MODEL, LANES), const),
            pl.BlockSpec((1, LANES), const),
        ],
        out_specs=[
            pl.BlockSpec((tm, D_MODEL), row),
            pl.BlockSpec((tm * ROW_TILES, LANES), row),
            pl.BlockSpec((tm, LANES), row),
            pl.BlockSpec((8, LANES), const),
        ],
        out_shape=[
            jax.ShapeDtypeStruct((n, D_MODEL), F32),
            jax.ShapeDtypeStruct((n * ROW_TILES, LANES), F32),
            jax.ShapeDtypeStruct((n, LANES), F32),
            jax.ShapeDtypeStruct((8, LANES), F32),
        ],
        scratch_shapes=[pltpu.VMEM((8, LANES), F32)],
        compiler_params=_cparams("arbitrary"),
        name="merge_router",
    )(x2, y_rw, y_sw, q_mem, memkv, memkv, anw, wg, wb0, wb1, wb2, wo, fnw, wrh, wrl, br)


def _w1_split_kernel(w_ref, g_ref, l_ref, t_ref):
    t_ref[...] = w_ref[...].T
    g_ref[...] = t_ref[pl.ds(0, D_FF, stride=2), :].T.astype(BF16)
    l_ref[...] = t_ref[pl.ds(1, D_FF, stride=2), :].T.astype(BF16)


def _w1_split(w1, rows):
    n_e = w1.shape[0]
    out = pl.BlockSpec((None, rows, D_FF), lambda e, r: (e, r, 0))
    return pl.pallas_call(
        _w1_split_kernel,
        grid=(n_e, D_MODEL // rows),
        in_specs=[pl.BlockSpec((None, rows, 2 * D_FF), lambda e, r: (e, r, 0))],
        out_specs=[out, out],
        out_shape=[jax.ShapeDtypeStruct((n_e, D_MODEL, D_FF), BF16)] * 2,
        scratch_shapes=[pltpu.VMEM((2 * D_FF, rows), F32)],
        compiler_params=_cparams("parallel", "parallel"),
        name="w1_split",
    )(w1)


IDX_RING = 4
ROW_BUFS = 2
MOE_EXTRA_BLOCKS = 2


def _moe_kernel(be_ref, idx_hbm, hn_hbm, w1g_ref, w1l_ref, b1g_ref, b1l_ref, w2_ref, b2_ref, y_hbm,
                idx_smem, xbuf, ybuf, gsem, ssem, isem):
    i = pl.program_id(0)
    last = pl.num_programs(0) - 1
    n_dump = y_hbm.shape[0] - 2 * EXPERT_BLOCK * ROW_TILES

    def tile_rows(start):
        return pl.ds(pl.multiple_of(start, ROW_TILES), ROW_TILES)

    def idx_copy(blk):
        ring = blk % IDX_RING
        return pltpu.make_async_copy(idx_hbm.at[blk], idx_smem.at[ring], isem.at[ring])

    def gather(blk, j):
        slot = blk % ROW_BUFS
        src = idx_smem[blk % IDX_RING, j]
        return pltpu.make_async_copy(hn_hbm.at[tile_rows(src), :],
                                     xbuf.at[slot, pl.ds(j * ROW_TILES, ROW_TILES), :], gsem.at[slot])

    def scatter(blk, j, dst=None):
        slot = blk % ROW_BUFS
        if dst is None:
            dst = idx_smem[blk % IDX_RING, EXPERT_BLOCK + j]
        return pltpu.make_async_copy(ybuf.at[slot, pl.ds(j * ROW_TILES, ROW_TILES), :],
                                     y_hbm.at[tile_rows(dst), :], ssem.at[slot])

    @pl.when(i == 0)
    def _():
        ybuf[...] = jnp.zeros_like(ybuf)
        for slot in range(ROW_BUFS):
            for j in range(EXPERT_BLOCK):
                scatter(slot, j, dst=n_dump + (slot * EXPERT_BLOCK + j) * ROW_TILES).start()
        idx_copy(0).start()
        idx_copy(0).wait()
        for j in range(EXPERT_BLOCK):
            gather(0, j).start()
        idx_copy(1).start()

    idx_copy(i + 1).wait()
    for j in range(EXPERT_BLOCK):
        gather(i + 1, j).start()
    for j in range(EXPERT_BLOCK):
        scatter(i, j, dst=0).wait()
    idx_copy(i + 2).start()
    for j in range(EXPERT_BLOCK):
        gather(i, j).wait()

    slot = i % ROW_BUFS
    xb = _load_row_tiles(xbuf.at[slot], EXPERT_BLOCK).astype(BF16)
    glu = _dot(xb, w1g_ref[...]) + b1g_ref[...]
    lin = _dot(xb, w1l_ref[...]) + b1l_ref[...]
    glu = jnp.minimum(glu, SWIGLU_LIMIT)
    lin = jnp.clip(lin, -SWIGLU_LIMIT, SWIGLU_LIMIT)
    act = glu * jax.nn.sigmoid(SWIGLU_ALPHA * glu) * (lin + 1.0)
    _store_row_tiles(ybuf.at[slot], _dot(act.astype(BF16), w2_ref[...]) + b2_ref[...])
    for j in range(EXPERT_BLOCK):
        scatter(i, j).start()

    @pl.when(i == last)
    def _():
        for blk in (i - 1, i):
            for j in range(EXPERT_BLOCK):
                scatter(blk, j, dst=0).wait()
        for j in range(EXPERT_BLOCK):
            gather(i + 1, j).wait()
        idx_copy(i + 2).wait()


def _moe_ffn(block_e, slot_idx, hn, w1g, w1l, b1g, b1l, w2, b2, n_rows):
    n_blocks = slot_idx.shape[0] - MOE_EXTRA_BLOCKS
    wspec = pl.BlockSpec((None, D_MODEL, D_FF), lambda i, be: (be[i], 0, 0))
    bspec = pl.BlockSpec((None, 1, D_FF), lambda i, be: (be[i], 0, 0))
    grid_spec = pltpu.PrefetchScalarGridSpec(
        num_scalar_prefetch=1,
        grid=(n_blocks,),
        in_specs=[
            pl.BlockSpec(memory_space=pl.ANY),
            pl.BlockSpec(memory_space=pl.ANY),
            wspec, wspec, bspec, bspec,
            pl.BlockSpec((None, D_FF, D_MODEL), lambda i, be: (be[i], 0, 0)),
            pl.BlockSpec((None, 1, D_MODEL), lambda i, be: (be[i], 0, 0)),
        ],
        out_specs=pl.BlockSpec(memory_space=pl.ANY),
        scratch_shapes=[
            pltpu.SMEM((IDX_RING, 2 * EXPERT_BLOCK), I32),
            pltpu.VMEM((ROW_BUFS, EXPERT_BLOCK * ROW_TILES, LANES), F32),
            pltpu.VMEM((ROW_BUFS, EXPERT_BLOCK * ROW_TILES, LANES), F32),
            pltpu.SemaphoreType.DMA((ROW_BUFS,)),
            pltpu.SemaphoreType.DMA((ROW_BUFS,)),
            pltpu.SemaphoreType.DMA((IDX_RING,)),
        ],
    )
    return pl.pallas_call(
        _moe_kernel,
        grid_spec=grid_spec,
        out_shape=jax.ShapeDtypeStruct(((n_rows + 2 * EXPERT_BLOCK) * ROW_TILES, LANES), F32),
        compiler_params=_cparams("arbitrary"),
        name="moe_ffn",
    )(block_e, slot_idx, hn, w1g, w1l, b1g, b1l, w2, b2)


def _combine_kernel(h_ref, y0_ref, y1_ref, y2_ref, y3_ref, meta_ref, fw_ref, o_ref):
    acc = h_ref[...]
    meta = meta_ref[...]
    for kk, y_ref in enumerate((y0_ref, y1_ref, y2_ref, y3_ref)):
        acc = acc + meta[:, TOP_K + kk:TOP_K + kk + 1] * _load_row_tiles(y_ref, acc.shape[0])
    o_ref[...] = _rms(acc, fw_ref[...])


def _combine(h, y_flat, meta, fw, tm):
    n = h.shape[0]
    tiles = n // tm
    row = lambda i: (i, 0)
    y_specs = [pl.BlockSpec((tm * ROW_TILES, LANES),
                            functools.partial(lambda i, kk: (kk * tiles + i, 0), kk=kk))
               for kk in range(TOP_K)]
    return pl.pallas_call(
        _combine_kernel,
        grid=(tiles,),
        in_specs=[pl.BlockSpec((tm, D_MODEL), row)] + y_specs + [
            pl.BlockSpec((tm, LANES), row),
            pl.BlockSpec((1, D_MODEL), lambda i: (0, 0)),
        ],
        out_specs=pl.BlockSpec((tm, D_MODEL), row),
        out_shape=jax.ShapeDtypeStruct((n, D_MODEL), F32),
        compiler_params=_cparams("parallel"),
        name="combine",
    )(h, y_flat, y_flat, y_flat, y_flat, meta, fw)


def _pad_heads_cols(w, heads):
    lead = w.shape[:-1]
    w = w.reshape(lead + (heads, HEAD_DIM))
    w = jnp.pad(w, [(0, 0)] * len(lead) + [(0, 0), (0, LANES - HEAD_DIM)])
    return w.reshape(lead + (heads * LANES,))


def _pick_tile(n, pref):
    t = pref
    while n % t:
        t //= 2
    return t


def kernel(x, mem, attn_norm_w, mem_norm_w, w_in, b_swa, rwkv_mu, rwkv_w0, rwkv_w2, rwkv_a0,
           rwkv_a2, rwkv_g2, rwkv_kk, rwkv_ka, rwkv_rk, rwkv_lnx_w, rwkv_lnx_b, swa_sinks,
           w_mem_kv, w_gate, w_branch, w_out, ffn_norm_w, w_router, b_router, w_exp1, b_exp1,
           w_exp2, b_exp2, final_norm_w):
    batch, seq, d = x.shape
    mem_len = mem.shape[1]
    n = batch * seq
    layer = 0
    x2 = x.reshape(n, d)
    row = lambda a: a.reshape(1, -1)

    w_in_l = w_in[layer]
    w_rw = w_in_l[:, :RWKV_PROJ].astype(BF16)
    sw = w_in_l[:, RWKV_PROJ:RWKV_PROJ + SWA_DIM + 2 * SWA_KV_DIM]
    bs = b_swa[layer]
    scale = HEAD_DIM ** -0.5
    w_sw = jnp.concatenate([
        _pad_heads_cols(sw[:, :SWA_DIM] * scale, SWA_Q_HEADS),
        _pad_heads_cols(sw[:, SWA_DIM:SWA_DIM + SWA_KV_DIM], SWA_KV_HEADS),
        _pad_heads_cols(sw[:, SWA_DIM + SWA_KV_DIM:], SWA_KV_HEADS)], axis=1).astype(BF16)
    b_sw = row(jnp.concatenate([
        _pad_heads_cols(bs[:SWA_DIM] * scale, SWA_Q_HEADS),
        _pad_heads_cols(bs[SWA_DIM:SWA_DIM + SWA_KV_DIM], SWA_KV_HEADS),
        _pad_heads_cols(bs[SWA_DIM + SWA_KV_DIM:], SWA_KV_HEADS)]))
    w_qm = w_in_l[:, RWKV_PROJ + SWA_DIM + 2 * SWA_KV_DIM:].astype(BF16)
    zeros_l = jnp.zeros((DECAY_LORA, RWKV_DIM), F32)
    wwa = jnp.concatenate([
        jnp.concatenate([rwkv_w2[layer], zeros_l], axis=1),
        jnp.concatenate([zeros_l, rwkv_a2[layer]], axis=1)], axis=0).astype(BF16)
    wb = w_branch[layer]
    wb1 = _pad_heads_cols(wb[1].T, SWA_Q_HEADS).T.astype(BF16)
    wr = jnp.pad(w_router[layer], ((0, 0), (0, LANES - N_EXPERTS)))
    wrh = wr.astype(BF16)
    wrl = (wr - wrh.astype(F32)).astype(BF16)
    br = row(jnp.pad(b_router[layer], (0, LANES - N_EXPERTS), constant_values=NEG_BIG))
    w1 = w_exp1[layer]
    b1 = b_exp1[layer]
    w1g, w1l = _w1_split(w1, LANES)
    b1g = b1[:, None, 0::2]
    b1l = b1[:, None, 1::2]
    w2 = w_exp2[layer].astype(BF16)
    b2 = b_exp2[layer][:, None, :]

    tm_proj = _pick_tile(seq, 512)
    p_rw, p_sw, q_mem = _inproj(x2, row(attn_norm_w[layer]), w_rw, w_sw, w_qm, b_sw, tm_proj)
    memkv = _memkv(mem.reshape(batch * mem_len, d), row(mem_norm_w[layer]),
                   w_mem_kv[layer].astype(BF16), mem_len)
    r, lw, k, v, kk, b, g, bonus = _rwkv_prep(
        p_rw, row(rwkv_mu[layer]), row(rwkv_w0[layer]), wwa, row(rwkv_a0[layer]),
        rwkv_g2[layer].astype(BF16), row(rwkv_kk[layer]), row(rwkv_ka[layer]),
        row(rwkv_rk[layer]), tm_proj, seq)
    y_rw = _rwkv_scan(r, lw, k, v, kk, b, g, bonus, row(rwkv_lnx_w[layer]),
                      row(rwkv_lnx_b[layer]), batch, seq, _pick_tile(seq, 256))
    y_sw = _swa(p_sw, swa_sinks[layer], batch, seq)

    tm_merge = _pick_tile(seq, 256)
    h, hn, meta, counts = _merge(
        x2, y_rw, y_sw, q_mem, memkv, row(attn_norm_w[layer]), w_gate[layer].astype(BF16),
        wb[0].astype(BF16), wb1, wb[2].astype(BF16), w_out[layer].astype(BF16),
        row(ffn_norm_w[layer]), wrh, wrl, br, tm_merge, seq, mem_len)

    a_total = n * TOP_K
    e_idx = meta[:, 0:TOP_K].astype(I32)
    pos = meta[:, 2 * TOP_K:3 * TOP_K].astype(I32)
    cnt = counts[0, :N_EXPERTS].astype(I32)
    padded = ((cnt + EXPERT_BLOCK - 1) // EXPERT_BLOCK) * EXPERT_BLOCK
    pad_ends = jnp.cumsum(padded)
    pad_starts = pad_ends - padded
    n_blocks = -(-a_total // EXPERT_BLOCK) + N_EXPERTS
    rows_idx = n_blocks + MOE_EXTRA_BLOCKS
    p_slots = rows_idx * EXPERT_BLOCK
    dest = (pad_starts[e_idx] + pos).reshape(-1)
    a_ids = jnp.arange(a_total, dtype=I32)
    slot_a = jnp.full((p_slots,), -1, I32).at[dest].set(a_ids)
    slot_id = jnp.arange(p_slots, dtype=I32)
    slot_tok = jnp.where(slot_a >= 0, slot_a // TOP_K, 0)
    dump = a_total + slot_id % (ROW_BUFS * EXPERT_BLOCK)
    slot_dst = jnp.where(slot_a >= 0, (slot_a % TOP_K) * n + slot_a // TOP_K, dump)
    slot_idx = jnp.concatenate([slot_tok.reshape(rows_idx, EXPERT_BLOCK),
                                slot_dst.reshape(rows_idx, EXPERT_BLOCK)], axis=1) * ROW_TILES
    block_start = jnp.arange(n_blocks, dtype=I32) * EXPERT_BLOCK
    block_e = jnp.minimum(jnp.sum(pad_ends[None, :] <= block_start[:, None], axis=1),
                          N_EXPERTS - 1).astype(I32)

    y_flat = _moe_ffn(block_e, slot_idx, hn, w1g, w1l, b1g, b1l, w2, b2, a_total)
    out = _combine(h, y_flat, meta, row(final_norm_w), tm_merge)
    return out.reshape(batch, seq, d)
```

```python
import functools

import jax
import jax.numpy as jnp
from jax import lax
from jax.experimental import pallas as pl
from jax.experimental.pallas import tpu as pltpu

F32 = jnp.float32
BF16 = jnp.bfloat16
I32 = jnp.int32

D_MODEL = 1024
HEAD_DIM = 64
LANES = 128
RWKV_DIM = 512
RWKV_PAIRS = RWKV_DIM // LANES
DECAY_LORA = 64
AAA_LORA = 64
GATE_LORA = 128
RWKV_PROJ = 3 * RWKV_DIM + DECAY_LORA + AAA_LORA + GATE_LORA
LNX_EPS = 64e-5
SWA_Q_HEADS = 8
SWA_KV_HEADS = 2
SWA_GROUP = SWA_Q_HEADS // SWA_KV_HEADS
SWA_DIM = SWA_Q_HEADS * HEAD_DIM
SWA_KV_DIM = SWA_KV_HEADS * HEAD_DIM
SWA_QP = SWA_Q_HEADS * LANES
SWA_KVP = SWA_KV_HEADS * LANES
SWA_PROJ_P = SWA_QP + 2 * SWA_KVP
Q_BLOCK = 128
ALIBI_MAX = 8.0
MEM_HEADS = 4
MEM_HEAD_DIM = 128
MEM_DIM = MEM_HEADS * MEM_HEAD_DIM
N_EXPERTS = 32
TOP_K = 4
D_FF = 1024
SWIGLU_ALPHA = 1.702
SWIGLU_LIMIT = 7.0
EXPERT_BLOCK = 128
NORM_EPS = 1e-5
CHUNK = 64
NEG_BIG = -1e30

VMEM_LIMIT = 56 * 1024 * 1024


def _cparams(*sem):
    return pltpu.CompilerParams(dimension_semantics=sem, vmem_limit_bytes=VMEM_LIMIT)


def _rms(x, w):
    return x * lax.rsqrt(jnp.mean(x * x, axis=-1, keepdims=True) + NORM_EPS) * w


def _dot(a, b):
    return jnp.dot(a, b, preferred_element_type=F32)


def _dot_nt(a, b):
    return lax.dot_general(a, b, (((1,), (1,)), ((), ())), preferred_element_type=F32)


def _split2(x):
    hi = x.astype(BF16)
    lo = (x - hi.astype(F32)).astype(BF16)
    return hi, lo


def _split3(x):
    hi = x.astype(BF16)
    r1 = x - hi.astype(F32)
    mid = r1.astype(BF16)
    lo = (r1 - mid.astype(F32)).astype(BF16)
    return hi, mid, lo


ROW_TILES = D_MODEL // LANES


def _store_row_tiles(ref, x):
    rows = x.shape[0]
    for c in range(ROW_TILES):
        ref[pl.ds(c, rows, stride=ROW_TILES), :] = x[:, c * LANES:(c + 1) * LANES]


def _load_row_tiles(ref, rows):
    return jnp.concatenate(
        [ref[pl.ds(c, rows, stride=ROW_TILES), :] for c in range(ROW_TILES)], axis=1)


def _group_ones(n, group):
    r = lax.broadcasted_iota(I32, (n, n), 0) // group
    c = lax.broadcasted_iota(I32, (n, n), 1) // group
    return (r == c).astype(BF16)


def _group_sum(x, ones):
    hi, lo = _split2(x)
    return _dot(hi, ones) + _dot(lo, ones)


def _inproj_kernel(x_ref, nw_ref, wr_ref, ws_ref, wq_ref, bs_ref, pr_ref, ps_ref, qm_ref):
    xb = _rms(x_ref[...], nw_ref[...]).astype(BF16)
    pr_ref[...] = _dot(xb, wr_ref[...])
    ps_ref[...] = (_dot(xb, ws_ref[...]) + bs_ref[...]).astype(BF16)
    qm_ref[...] = _dot(xb, wq_ref[...]).astype(BF16)


def _inproj(x2, nw, w_rw, w_sw, w_qm, b_sw, tm):
    n = x2.shape[0]
    const = lambda i: (0, 0)
    row = lambda i: (i, 0)
    return pl.pallas_call(
        _inproj_kernel,
        grid=(n // tm,),
        in_specs=[
            pl.BlockSpec((tm, D_MODEL), row),
            pl.BlockSpec((1, D_MODEL), const),
            pl.BlockSpec((D_MODEL, RWKV_PROJ), const),
            pl.BlockSpec((D_MODEL, SWA_PROJ_P), const),
            pl.BlockSpec((D_MODEL, MEM_DIM), const),
            pl.BlockSpec((1, SWA_PROJ_P), const),
        ],
        out_specs=[
            pl.BlockSpec((tm, RWKV_PROJ), row),
            pl.BlockSpec((tm, SWA_PROJ_P), row),
            pl.BlockSpec((tm, MEM_DIM), row),
        ],
        out_shape=[
            jax.ShapeDtypeStruct((n, RWKV_PROJ), F32),
            jax.ShapeDtypeStruct((n, SWA_PROJ_P), BF16),
            jax.ShapeDtypeStruct((n, MEM_DIM), BF16),
        ],
        compiler_params=_cparams("parallel"),
        name="inproj",
    )(x2, nw, w_rw, w_sw, w_qm, b_sw)


def _memkv_kernel(m_ref, nw_ref, w_ref, o_ref):
    mb = _rms(m_ref[...], nw_ref[...]).astype(BF16)
    o_ref[...] = _dot(mb, w_ref[...]).astype(BF16)


def _memkv(mem2, nw, w, tm):
    n = mem2.shape[0]
    return pl.pallas_call(
        _memkv_kernel,
        grid=(n // tm,),
        in_specs=[
            pl.BlockSpec((tm, D_MODEL), lambda i: (i, 0)),
            pl.BlockSpec((1, D_MODEL), lambda i: (0, 0)),
            pl.BlockSpec((D_MODEL, 2 * MEM_DIM), lambda i: (0, 0)),
        ],
        out_specs=pl.BlockSpec((tm, 2 * MEM_DIM), lambda i: (i, 0)),
        out_shape=jax.ShapeDtypeStruct((n, 2 * MEM_DIM), BF16),
        compiler_params=_cparams("parallel"),
        name="memkv",
    )(mem2, nw, w)


def _rwkv_prep_kernel(p_ref, halo_ref, mu_ref, w0_ref, wwa_ref, a0_ref, g2_ref, kk_ref, ka_ref,
                      rk_ref, r_out, lw_out, k_out, v_out, kk_out, b_out, g_out, bonus_out,
                      *, tiles_per_seq):
    i = pl.program_id(0)
    p = p_ref[...]
    tm = p.shape[0]
    halo = jnp.where(i % tiles_per_seq == 0, 0.0, halo_ref[7:8, :])
    row = lax.broadcasted_iota(I32, p.shape, 0)
    prev = jnp.where(row == 0, halo, pltpu.roll(p, 1, axis=0))
    ps = p + (prev - p) * mu_ref[...]
    r = ps[:, 0:RWKV_DIM]
    k = ps[:, RWKV_DIM:2 * RWKV_DIM]
    v = ps[:, 2 * RWKV_DIM:3 * RWKV_DIM]
    wa = ps[:, 3 * RWKV_DIM:3 * RWKV_DIM + LANES]
    pg = ps[:, 3 * RWKV_DIM + LANES:]
    lane = lax.broadcasted_iota(I32, (tm, LANES), 1)
    z = jnp.where(lane < DECAY_LORA, jnp.tanh(wa), wa).astype(BF16)
    lora = _dot(z, wwa_ref[...])
    w_log = -jax.nn.softplus(-(w0_ref[...] + lora[:, :RWKV_DIM])) - 0.5
    lw_out[...] = -jnp.exp(w_log)
    alr = jax.nn.sigmoid(a0_ref[...] + lora[:, RWKV_DIM:])
    g_out[...] = _dot(jax.nn.sigmoid(pg).astype(BF16), g2_ref[...])
    ones = _group_ones(RWKV_DIM, HEAD_DIM)
    kk = k * kk_ref[...]
    kk = kk / jnp.maximum(jnp.sqrt(_group_sum(kk * kk, ones)), 1e-12)
    k2 = k * (1.0 + (alr - 1.0) * ka_ref[...])
    r_out[...] = r
    k_out[...] = k2
    v_out[...] = v
    kk_out[...] = kk
    b_out[...] = kk * alr
    bonus_out[...] = _group_sum(r * k2 * rk_ref[...], ones) * v


def _rwkv_prep(p_rw, mu, w0, wwa, a0, g2, k_k, k_a, r_k, tm, seq):
    n = p_rw.shape[0]
    tiles_per_seq = seq // tm
    const = lambda i: (0, 0)
    row = lambda i: (i, 0)
    vec = pl.BlockSpec((1, RWKV_DIM), const)
    out = pl.BlockSpec((tm, RWKV_DIM), row)
    return pl.pallas_call(
        functools.partial(_rwkv_prep_kernel, tiles_per_seq=tiles_per_seq),
        grid=(n // tm,),
        in_specs=[
            pl.BlockSpec((tm, RWKV_PROJ), row),
            pl.BlockSpec((8, RWKV_PROJ), lambda i: (jnp.maximum(i * (tm // 8) - 1, 0), 0)),
            pl.BlockSpec((1, RWKV_PROJ), const),
            vec,
            pl.BlockSpec((LANES, 2 * RWKV_DIM), const),
            vec,
            pl.BlockSpec((GATE_LORA, RWKV_DIM), const),
            vec, vec, vec,
        ],
        out_specs=[out] * 8,
        out_shape=[jax.ShapeDtypeStruct((n, RWKV_DIM), F32)] * 8,
        compiler_params=_cparams("parallel"),
        name="rwkv_prep",
    )(p_rw, p_rw, mu, w0, wwa, a0, g2, k_k, k_a, r_k)


def _blockdiag(x, first_head):
    zero = jnp.zeros_like(x)
    return jnp.concatenate([jnp.where(first_head, x, zero), jnp.where(first_head, zero, x)], axis=0)


def _rwkv_scan_kernel(r_ref, lw_ref, k_ref, v_ref, kk_ref, b_ref, g_ref, bonus_ref, lnw_ref,
                      lnb_ref, y_ref, s_ref, *, chunks):
    @pl.when(pl.program_id(1) == 0)
    def _():
        s_ref[...] = jnp.zeros_like(s_ref)

    c2 = 2 * CHUNK
    tri_incl = (lax.broadcasted_iota(I32, (CHUNK, CHUNK), 1)
                <= lax.broadcasted_iota(I32, (CHUNK, CHUNK), 0)).astype(BF16)
    rowi = lax.broadcasted_iota(I32, (c2, c2), 0)
    coli = lax.broadcasted_iota(I32, (c2, c2), 1)
    strict = coli < rowi
    incl = coli <= rowi
    eye = (coli == rowi).astype(F32)
    first_head = lax.broadcasted_iota(I32, (CHUNK, LANES), 1) < HEAD_DIM
    ones = _group_ones(RWKV_DIM, HEAD_DIM)
    inv_n = 1.0 / HEAD_DIM

    seqs = r_ref.shape[0]
    pairs = range(RWKV_PAIRS)
    lanes = [slice(j * LANES, (j + 1) * LANES) for j in pairs]
    units = [(q, j) for q in range(seqs) for j in pairs]

    def chunk_body(c, carry):
        rows = pl.ds(pl.multiple_of(c * CHUNK, CHUNK), CHUNK)
        a_t, r_t, b_t, k_t, b_h, k_h, p_end, v = [], [], [], [], [], [], [], []
        for q in range(seqs):
            lw = lw_ref[q, rows, :]
            hi, mid, lo = _split3(lw)
            cl = _dot(tri_incl, hi) + _dot(tri_incl, mid) + _dot(tri_incl, lo)
            cl_end = cl[CHUNK - 1:CHUNK, :]
            kq = k_ref[q, rows, :]
            bq = b_ref[q, rows, :]
            inv = jnp.exp(-cl)
            to_end = jnp.exp(cl_end - cl)
            a_t.append(-kk_ref[q, rows, :] * jnp.exp(cl - lw))
            r_t.append(r_ref[q, rows, :] * jnp.exp(cl))
            b_t.append(bq * inv)
            k_t.append(kq * inv)
            b_h.append(bq * to_end)
            k_h.append(kq * to_end)
            p_end.append(jnp.exp(cl_end))
            v.append(v_ref[q, rows, :])
        bd = lambda x, u: _blockdiag(x[u[0]][:, lanes[u[1]]], first_head).astype(BF16)
        lhs = [jnp.concatenate([bd(a_t, u), bd(r_t, u)], axis=0) for u in units]
        a4 = [_dot_nt(lhs[i], jnp.concatenate([bd(b_t, u), bd(k_t, u)], axis=0))
              for i, u in enumerate(units)]
        ids = range(len(units))
        a_ab = [jnp.where(strict, a4[i][:c2, :c2], 0.0) for i in ids]
        tinv = [eye + a_ab[i] for i in ids]
        ab = [a_ab[i].astype(BF16) for i in ids]
        apow = [_dot(ab[i], ab[i]) for i in ids]
        for _ in range(4):
            st = [_dot(jnp.concatenate([tinv[i], apow[i]], axis=0).astype(BF16),
                       apow[i].astype(BF16)) for i in ids]
            tinv = [tinv[i] + st[i][:c2] for i in ids]
            apow = [st[i][c2:] for i in ids]
        tinv = [(tinv[i] + _dot(tinv[i].astype(BF16), apow[i].astype(BF16))).astype(BF16)
                for i in ids]
        v2 = [bd(v, u) for u in units]
        a_ak = [jnp.where(strict, a4[i][:c2, c2:], 0.0).astype(BF16) for i in ids]
        a_r = [jnp.concatenate([jnp.where(incl, a4[i][c2:, :c2], 0.0),
                                jnp.where(incl, a4[i][c2:, c2:], 0.0)], axis=1).astype(BF16)
               for i in ids]
        akv = [_dot(a_ak[i], v2[i]) for i in ids]
        s_old = [s_ref[i] for i in ids]
        su = [_dot_nt(lhs[i], s_old[i].astype(BF16)) for i in ids]
        sa = [_dot(tinv[i], (su[i][:c2] + akv[i]).astype(BF16)) for i in ids]
        y2 = [su[i][c2:] + _dot(a_r[i], jnp.concatenate([sa[i].astype(BF16), v2[i]], axis=0))
              for i in ids]
        sav_t = [jnp.concatenate([sa[i], v2[i].astype(F32)], axis=0).T.astype(BF16) for i in ids]
        for i, u in enumerate(units):
            s_ref[i] = s_old[i] * p_end[u[0]][:, lanes[u[1]]] + _dot(
                sav_t[i], jnp.concatenate([bd(b_h, u), bd(k_h, u)], axis=0))
        for q in range(seqs):
            y = jnp.concatenate([y2[q * RWKV_PAIRS + j][:CHUNK] + y2[q * RWKV_PAIRS + j][CHUNK:]
                                 for j in pairs], axis=1)
            m = _group_sum(y, ones) * inv_n
            d = y - m
            var = _group_sum(d * d, ones) * inv_n
            yn = d * lax.rsqrt(var + LNX_EPS) * lnw_ref[...] + lnb_ref[...]
            y_ref[q, rows, :] = ((yn + bonus_ref[q, rows, :]) * g_ref[q, rows, :]).astype(BF16)
        return carry

    lax.fori_loop(0, chunks, chunk_body, 0)


def _rwkv_scan(r, lw, k, v, kk, b, g, bonus, lnw, lnb, batch, seq, tc, seqs):
    as3 = lambda a: a.reshape(batch, seq, RWKV_DIM)
    blk = pl.BlockSpec((seqs, tc, RWKV_DIM), lambda bi, ti: (bi, ti, 0))
    vec = pl.BlockSpec((1, RWKV_DIM), lambda bi, ti: (0, 0))
    y = pl.pallas_call(
        functools.partial(_rwkv_scan_kernel, chunks=tc // CHUNK),
        grid=(batch // seqs, seq // tc),
        in_specs=[blk] * 8 + [vec, vec],
        out_specs=blk,
        out_shape=jax.ShapeDtypeStruct((batch, seq, RWKV_DIM), BF16),
        scratch_shapes=[pltpu.VMEM((seqs * RWKV_PAIRS, LANES, LANES), F32)],
        compiler_params=_cparams("parallel", "arbitrary"),
        name="rwkv_scan",
    )(*(as3(a) for a in (r, lw, k, v, kk, b, g, bonus)), lnw, lnb)
    return y.reshape(batch * seq, RWKV_DIM)


def _swa_kernel(sink_ref, q_ref, kvc_ref, kvp_ref, o_ref):
    nblk = pl.program_id(1)
    kv = jnp.concatenate([kvp_ref[...], kvc_ref[...]], axis=0)
    i = lax.broadcasted_iota(I32, (Q_BLOCK, 2 * Q_BLOCK), 0)
    j = lax.broadcasted_iota(I32, (Q_BLOCK, 2 * Q_BLOCK), 1)
    dist = i + Q_BLOCK - j
    valid = (dist >= 0) & (dist < Q_BLOCK) & ((j >= Q_BLOCK) | (nblk > 0))
    distf = dist.astype(F32)
    for h in range(SWA_KV_HEADS):
        kh = kv[:, h * LANES:(h + 1) * LANES]
        vh = kv[:, SWA_KVP + h * LANES:SWA_KVP + (h + 1) * LANES]
        for gi in range(SWA_GROUP):
            hq = h * SWA_GROUP + gi
            cols = slice(hq * LANES, (hq + 1) * LANES)
            slope = 2.0 ** (-ALIBI_MAX * (hq + 1) / SWA_Q_HEADS)
            s = _dot_nt(q_ref[:, cols], kh) - slope * distf
            s = jnp.where(valid, s, NEG_BIG)
            sink = sink_ref[hq]
            m = jnp.maximum(jnp.max(s, axis=-1, keepdims=True), sink)
            e = jnp.exp(s - m)
            den = jnp.sum(e, axis=-1, keepdims=True) + jnp.exp(sink - m)
            o_ref[:, cols] = _dot((e / den).astype(BF16), vh).astype(BF16)


def _swa(p_sw, sinks, batch, seq):
    n = p_sw.shape[0]
    nb = seq // Q_BLOCK
    kv_col = SWA_QP // (2 * SWA_KVP)
    return pl.pallas_call(
        _swa_kernel,
        grid=(batch, nb),
        in_specs=[
            pl.BlockSpec(memory_space=pltpu.SMEM),
            pl.BlockSpec((Q_BLOCK, SWA_QP), lambda bi, ni: (bi * nb + ni, 0)),
            pl.BlockSpec((Q_BLOCK, 2 * SWA_KVP), lambda bi, ni: (bi * nb + ni, kv_col)),
            pl.BlockSpec((Q_BLOCK, 2 * SWA_KVP),
                         lambda bi, ni: (bi * nb + jnp.maximum(ni - 1, 0), kv_col)),
        ],
        out_specs=pl.BlockSpec((Q_BLOCK, SWA_QP), lambda bi, ni: (bi * nb + ni, 0)),
        out_shape=jax.ShapeDtypeStruct((n, SWA_QP), BF16),
        compiler_params=_cparams("parallel", "parallel"),
        name="swa",
    )(sinks, p_sw, p_sw, p_sw)


def _merge_kernel(x_ref, yr_ref, ys_ref, qm_ref, km_ref, vm_ref, anw_ref, wg_ref, wb0_ref, wb1_ref,
                  wb2_ref, wo_ref, fnw_ref, wrh_ref, wrl_ref, br_ref,
                  h_ref, hn_ref, meta_ref, cnt_ref, carry_ref):
    @pl.when(pl.program_id(0) == 0)
    def _():
        carry_ref[...] = jnp.zeros_like(carry_ref)

    x = x_ref[...]
    tm = x.shape[0]
    xb = _rms(x, anw_ref[...]).astype(BF16)
    outs = []
    for hh in range(MEM_HEADS):
        cols = slice(hh * MEM_HEAD_DIM, (hh + 1) * MEM_HEAD_DIM)
        s = _dot_nt(qm_ref[:, cols], km_ref[:, cols]) * (MEM_HEAD_DIM ** -0.5)
        m = jnp.max(s, axis=-1, keepdims=True)
        e = jnp.exp(s - m)
        pr = (e / jnp.sum(e, axis=-1, keepdims=True)).astype(BF16)
        outs.append(_dot(pr, vm_ref[:, cols]))
    y_mem = jnp.concatenate(outs, axis=1).astype(BF16)
    merged = jax.nn.sigmoid(_dot(xb, wg_ref[0])) * _dot(yr_ref[...], wb0_ref[...])
    merged += jax.nn.sigmoid(_dot(xb, wg_ref[1])) * _dot(ys_ref[...], wb1_ref[...])
    merged += jax.nn.sigmoid(_dot(xb, wg_ref[2])) * _dot(y_mem, wb2_ref[...])
    h = x + _dot(merged.astype(BF16), wo_ref[...])
    h_ref[...] = h
    hn = _rms(h, fnw_ref[...])
    _store_row_tiles(hn_ref, hn)
    hi, lo = _split2(hn)
    logits = _dot(hi, wrh_ref[...]) + _dot(hi, wrl_ref[...]) + _dot(lo, wrh_ref[...]) + br_ref[...]
    lane = lax.broadcasted_iota(I32, (tm, LANES), 1).astype(F32)
    work = logits
    vals, idxs, sels = [], [], []
    for _ in range(TOP_K):
        mk = jnp.max(work, axis=-1, keepdims=True)
        ik = jnp.min(jnp.where(work == mk, lane, float(LANES)), axis=-1, keepdims=True)
        sel = lane == ik
        work = jnp.where(sel, 2.0 * NEG_BIG, work)
        vals.append(mk)
        idxs.append(ik)
        sels.append(sel)
    es = [jnp.exp(vk - vals[0]) for vk in vals]
    den = es[0] + es[1] + es[2] + es[3]
    cnt = jnp.zeros((tm, LANES), F32)
    for sel in sels:
        cnt = cnt + sel.astype(F32)
    tri = (lax.broadcasted_iota(I32, (tm, tm), 1) < lax.broadcasted_iota(I32, (tm, tm), 0)).astype(BF16)
    prefix = _dot(tri, cnt.astype(BF16)) + carry_ref[0:1, :]
    meta = jnp.zeros((tm, LANES), F32)
    for kk in range(TOP_K):
        pos = jnp.sum(jnp.where(sels[kk], prefix, 0.0), axis=-1, keepdims=True)
        meta = jnp.where(lane == kk, idxs[kk], meta)
        meta = jnp.where(lane == TOP_K + kk, es[kk] / den, meta)
        meta = jnp.where(lane == 2 * TOP_K + kk, pos, meta)
    meta_ref[...] = meta
    total = carry_ref[0:1, :] + jnp.sum(cnt, axis=0, keepdims=True)
    carry_ref[...] = jnp.broadcast_to(total, carry_ref.shape)
    cnt_ref[...] = jnp.broadcast_to(total, cnt_ref.shape)


def _merge(x2, y_rw, y_sw, q_mem, memkv, anw, wg, wb0, wb1, wb2, wo, fnw, wrh, wrl, br, tm, seq,
           mem_len):
    n = x2.shape[0]
    tiles_per_seq = seq // tm
    row = lambda i: (i, 0)
    const = lambda i: (0, 0)
    return pl.pallas_call(
        _merge_kernel,
        grid=(n // tm,),
        in_specs=[
            pl.BlockSpec((tm, D_MODEL), row),
            pl.BlockSpec((tm, RWKV_DIM), row),
            pl.BlockSpec((tm, SWA_QP), row),
            pl.BlockSpec((tm, MEM_DIM), row),
            pl.BlockSpec((mem_len, MEM_DIM), lambda i: (i // tiles_per_seq, 0)),
            pl.BlockSpec((mem_len, MEM_DIM), lambda i: (i // tiles_per_seq, 1)),
            pl.BlockSpec((1, D_MODEL), const),
            pl.BlockSpec((3, D_MODEL, D_MODEL), lambda i: (0, 0, 0)),
            pl.BlockSpec((RWKV_DIM, D_MODEL), const),
            pl.BlockSpec((SWA_QP, D_MODEL), const),
            pl.BlockSpec((MEM_DIM, D_MODEL), const),
            pl.BlockSpec((D_MODEL, D_MODEL), const),
            pl.BlockSpec((1, D_MODEL), const),
            pl.BlockSpec((D_MODEL, LANES), const),
            pl.BlockSpec((D_MODEL, LANES), const),
            pl.BlockSpec((1, LANES), const),
        ],
        out_specs=[
            pl.BlockSpec((tm, D_MODEL), row),
            pl.BlockSpec((tm * ROW_TILES, LANES), row),
            pl.BlockSpec((tm, LANES), row),
            pl.BlockSpec((8, LANES), const),
        ],
        out_shape=[
            jax.ShapeDtypeStruct((n, D_MODEL), F32),
            jax.ShapeDtypeStruct((n * ROW_TILES, LANES), F32),
            jax.ShapeDtypeStruct((n, LANES), F32),
            jax.ShapeDtypeStruct((8, LANES), F32),
        ],
        scratch_shapes=[pltpu.VMEM((8, LANES), F32)],
        compiler_params=_cparams("arbitrary"),
        name="merge_router",
    )(x2, y_rw, y_sw, q_mem, memkv, memkv, anw, wg, wb0, wb1, wb2, wo, fnw, wrh, wrl, br)


def _w1_split_kernel(w_ref, g_ref, l_ref, t_ref):
    t_ref[...] = w_ref[...].T
    g_ref[...] = t_ref[pl.ds(0, D_FF, stride=2), :].T.astype(BF16)
    l_ref[...] = t_ref[pl.ds(1, D_FF, stride=2), :].T.astype(BF16)


def _w1_split(w1, rows):
    n_e = w1.shape[0]
    out = pl.BlockSpec((None, rows, D_FF), lambda e, r: (e, r, 0))
    return pl.pallas_call(
        _w1_split_kernel,
        grid=(n_e, D_MODEL // rows),
        in_specs=[pl.BlockSpec((None, rows, 2 * D_FF), lambda e, r: (e, r, 0))],
        out_specs=[out, out],
        out_shape=[jax.ShapeDtypeStruct((n_e, D_MODEL, D_FF), BF16)] * 2,
        scratch_shapes=[pltpu.VMEM((2 * D_FF, rows), F32)],
        compiler_params=_cparams("parallel", "parallel"),
        name="w1_split",
    )(w1)


IDX_RING = 4
ROW_BUFS = 2
MOE_EXTRA_BLOCKS = 2


def _moe_kernel(be_ref, idx_hbm, hn_hbm, w1g_ref, w1l_ref, b1g_ref, b1l_ref, w2_ref, b2_ref, y_hbm,
                idx_smem, xbuf, ybuf, gsem, ssem, isem):
    i = pl.program_id(0)
    last = pl.num_programs(0) - 1
    n_dump = y_hbm.shape[0] - 2 * EXPERT_BLOCK * ROW_TILES

    def tile_rows(start):
        return pl.ds(pl.multiple_of(start, ROW_TILES), ROW_TILES)

    def idx_copy(blk):
        ring = blk % IDX_RING
        return pltpu.make_async_copy(idx_hbm.at[blk], idx_smem.at[ring], isem.at[ring])

    def gather(blk, j):
        slot = blk % ROW_BUFS
        src = idx_smem[blk % IDX_RING, j]
        return pltpu.make_async_copy(hn_hbm.at[tile_rows(src), :],
                                     xbuf.at[slot, pl.ds(j * ROW_TILES, ROW_TILES), :], gsem.at[slot])

    def scatter(blk, j, dst=None):
        slot = blk % ROW_BUFS
        if dst is None:
            dst = idx_smem[blk % IDX_RING, EXPERT_BLOCK + j]
        return pltpu.make_async_copy(ybuf.at[slot, pl.ds(j * ROW_TILES, ROW_TILES), :],
                                     y_hbm.at[tile_rows(dst), :], ssem.at[slot])

    @pl.when(i == 0)
    def _():
        ybuf[...] = jnp.zeros_like(ybuf)
        for slot in range(ROW_BUFS):
            for j in range(EXPERT_BLOCK):
                scatter(slot, j, dst=n_dump + (slot * EXPERT_BLOCK + j) * ROW_TILES).start()
        idx_copy(0).start()
        idx_copy(0).wait()
        for j in range(EXPERT_BLOCK):
            gather(0, j).start()
        idx_copy(1).start()

    idx_copy(i + 1).wait()
    for j in range(EXPERT_BLOCK):
        gather(i + 1, j).start(priority=j % 2)
    for j in range(EXPERT_BLOCK):
        scatter(i, j, dst=0).wait()
    idx_copy(i + 2).start()
    for j in range(EXPERT_BLOCK):
        gather(i, j).wait()

    slot = i % ROW_BUFS
    xb = _load_row_tiles(xbuf.at[slot], EXPERT_BLOCK).astype(BF16)
    glu = _dot(xb, w1g_ref[...]) + b1g_ref[...]
    lin = _dot(xb, w1l_ref[...]) + b1l_ref[...]
    glu = jnp.minimum(glu, SWIGLU_LIMIT)
    lin = jnp.clip(lin, -SWIGLU_LIMIT, SWIGLU_LIMIT)
    act = glu * jax.nn.sigmoid(SWIGLU_ALPHA * glu) * (lin + 1.0)
    _store_row_tiles(ybuf.at[slot], _dot(act.astype(BF16), w2_ref[...]) + b2_ref[...])
    for j in range(EXPERT_BLOCK):
        scatter(i, j).start(priority=j % 2)

    @pl.when(i == last)
    def _():
        for blk in (i - 1, i):
            for j in range(EXPERT_BLOCK):
                scatter(blk, j, dst=0).wait()
        for j in range(EXPERT_BLOCK):
            gather(i + 1, j).wait()
        idx_copy(i + 2).wait()


def _moe_ffn(block_e, slot_idx, hn, w1g, w1l, b1g, b1l, w2, b2, n_rows):
    n_blocks = slot_idx.shape[0] - MOE_EXTRA_BLOCKS
    wspec = pl.BlockSpec((None, D_MODEL, D_FF), lambda i, be: (be[i], 0, 0))
    bspec = pl.BlockSpec((None, 1, D_FF), lambda i, be: (be[i], 0, 0))
    grid_spec = pltpu.PrefetchScalarGridSpec(
        num_scalar_prefetch=1,
        grid=(n_blocks,),
        in_specs=[
            pl.BlockSpec(memory_space=pl.ANY),
            pl.BlockSpec(memory_space=pl.ANY),
            wspec, wspec, bspec, bspec,
            pl.BlockSpec((None, D_FF, D_MODEL), lambda i, be: (be[i], 0, 0)),
            pl.BlockSpec((None, 1, D_MODEL), lambda i, be: (be[i], 0, 0)),
        ],
        out_specs=pl.BlockSpec(memory_space=pl.ANY),
        scratch_shapes=[
            pltpu.SMEM((IDX_RING, 2 * EXPERT_BLOCK), I32),
            pltpu.VMEM((ROW_BUFS, EXPERT_BLOCK * ROW_TILES, LANES), F32),
            pltpu.VMEM((ROW_BUFS, EXPERT_BLOCK * ROW_TILES, LANES), F32),
            pltpu.SemaphoreType.DMA((ROW_BUFS,)),
            pltpu.SemaphoreType.DMA((ROW_BUFS,)),
            pltpu.SemaphoreType.DMA((IDX_RING,)),
        ],
    )
    return pl.pallas_call(
        _moe_kernel,
        grid_spec=grid_spec,
        out_shape=jax.ShapeDtypeStruct(((n_rows + 2 * EXPERT_BLOCK) * ROW_TILES, LANES), F32),
        compiler_params=_cparams("arbitrary"),
        name="moe_ffn",
    )(block_e, slot_idx, hn, w1g, w1l, b1g, b1l, w2, b2)


def _combine_kernel(h_ref, y0_ref, y1_ref, y2_ref, y3_ref, meta_ref, fw_ref, o_ref):
    acc = h_ref[...]
    meta = meta_ref[...]
    for kk, y_ref in enumerate((y0_ref, y1_ref, y2_ref, y3_ref)):
        acc = acc + meta[:, TOP_K + kk:TOP_K + kk + 1] * _load_row_tiles(y_ref, acc.shape[0])
    o_ref[...] = _rms(acc, fw_ref[...])


def _combine(h, y_flat, meta, fw, tm):
    n = h.shape[0]
    tiles = n // tm
    row = lambda i: (i, 0)
    y_specs = [pl.BlockSpec((tm * ROW_TILES, LANES),
                            functools.partial(lambda i, kk: (kk * tiles + i, 0), kk=kk))
               for kk in range(TOP_K)]
    return pl.pallas_call(
        _combine_kernel,
        grid=(tiles,),
        in_specs=[pl.BlockSpec((tm, D_MODEL), row)] + y_specs + [
            pl.BlockSpec((tm, LANES), row),
            pl.BlockSpec((1, D_MODEL), lambda i: (0, 0)),
        ],
        out_specs=pl.BlockSpec((tm, D_MODEL), row),
        out_shape=jax.ShapeDtypeStruct((n, D_MODEL), F32),
        compiler_params=_cparams("parallel"),
        name="combine",
    )(h, y_flat, y_flat, y_flat, y_flat, meta, fw)


def _pad_heads_cols(w, heads):
    lead = w.shape[:-1]
    w = w.reshape(lead + (heads, HEAD_DIM))
    w = jnp.pad(w, [(0, 0)] * len(lead) + [(0, 0), (0, LANES - HEAD_DIM)])
    return w.reshape(lead + (heads * LANES,))


def _pick_tile(n, pref):
    t = pref
    while n % t:
        t //= 2
    return t


def kernel(x, mem, attn_norm_w, mem_norm_w, w_in, b_swa, rwkv_mu, rwkv_w0, rwkv_w2, rwkv_a0,
           rwkv_a2, rwkv_g2, rwkv_kk, rwkv_ka, rwkv_rk, rwkv_lnx_w, rwkv_lnx_b, swa_sinks,
           w_mem_kv, w_gate, w_branch, w_out, ffn_norm_w, w_router, b_router, w_exp1, b_exp1,
           w_exp2, b_exp2, final_norm_w):
    batch, seq, d = x.shape
    mem_len = mem.shape[1]
    n = batch * seq
    layer = 0
    x2 = x.reshape(n, d)
    row = lambda a: a.reshape(1, -1)

    w_in_l = w_in[layer]
    w_rw = w_in_l[:, :RWKV_PROJ].astype(BF16)
    sw = w_in_l[:, RWKV_PROJ:RWKV_PROJ + SWA_DIM + 2 * SWA_KV_DIM]
    bs = b_swa[layer]
    scale = HEAD_DIM ** -0.5
    w_sw = jnp.concatenate([
        _pad_heads_cols(sw[:, :SWA_DIM] * scale, SWA_Q_HEADS),
        _pad_heads_cols(sw[:, SWA_DIM:SWA_DIM + SWA_KV_DIM], SWA_KV_HEADS),
        _pad_heads_cols(sw[:, SWA_DIM + SWA_KV_DIM:], SWA_KV_HEADS)], axis=1).astype(BF16)
    b_sw = row(jnp.concatenate([
        _pad_heads_cols(bs[:SWA_DIM] * scale, SWA_Q_HEADS),
        _pad_heads_cols(bs[SWA_DIM:SWA_DIM + SWA_KV_DIM], SWA_KV_HEADS),
        _pad_heads_cols(bs[SWA_DIM + SWA_KV_DIM:], SWA_KV_HEADS)]))
    w_qm = w_in_l[:, RWKV_PROJ + SWA_DIM + 2 * SWA_KV_DIM:].astype(BF16)
    zeros_l = jnp.zeros((DECAY_LORA, RWKV_DIM), F32)
    wwa = jnp.concatenate([
        jnp.concatenate([rwkv_w2[layer], zeros_l], axis=1),
        jnp.concatenate([zeros_l, rwkv_a2[layer]], axis=1)], axis=0).astype(BF16)
    wb = w_branch[layer]
    wb1 = _pad_heads_cols(wb[1].T, SWA_Q_HEADS).T.astype(BF16)
    wr = jnp.pad(w_router[layer], ((0, 0), (0, LANES - N_EXPERTS)))
    wrh = wr.astype(BF16)
    wrl = (wr - wrh.astype(F32)).astype(BF16)
    br = row(jnp.pad(b_router[layer], (0, LANES - N_EXPERTS), constant_values=NEG_BIG))
    w1 = w_exp1[layer]
    b1 = b_exp1[layer]
    w1g, w1l = _w1_split(w1, LANES)
    b1g = b1[:, None, 0::2]
    b1l = b1[:, None, 1::2]
    w2 = w_exp2[layer].astype(BF16)
    b2 = b_exp2[layer][:, None, :]

    tm_proj = _pick_tile(seq, 512)
    p_rw, p_sw, q_mem = _inproj(x2, row(attn_norm_w[layer]), w_rw, w_sw, w_qm, b_sw, tm_proj)
    memkv = _memkv(mem.reshape(batch * mem_len, d), row(mem_norm_w[layer]),
                   w_mem_kv[layer].astype(BF16), mem_len)
    r, lw, k, v, kk, b, g, bonus = _rwkv_prep(
        p_rw, row(rwkv_mu[layer]), row(rwkv_w0[layer]), wwa, row(rwkv_a0[layer]),
        rwkv_g2[layer].astype(BF16), row(rwkv_kk[layer]), row(rwkv_ka[layer]),
        row(rwkv_rk[layer]), tm_proj, seq)
    y_rw = _rwkv_scan(r, lw, k, v, kk, b, g, bonus, row(rwkv_lnx_w[layer]),
                      row(rwkv_lnx_b[layer]), batch, seq, _pick_tile(seq, 256),
                      4 if batch % 4 == 0 else 1)
    y_sw = _swa(p_sw, swa_sinks[layer], batch, seq)

    tm_merge = _pick_tile(seq, 256)
    h, hn, meta, counts = _merge(
        x2, y_rw, y_sw, q_mem, memkv, row(attn_norm_w[layer]), w_gate[layer].astype(BF16),
        wb[0].astype(BF16), wb1, wb[2].astype(BF16), w_out[layer].astype(BF16),
        row(ffn_norm_w[layer]), wrh, wrl, br, tm_merge, seq, mem_len)

    a_total = n * TOP_K
    e_idx = meta[:, 0:TOP_K].astype(I32)
    pos = meta[:, 2 * TOP_K:3 * TOP_K].astype(I32)
    cnt = counts[0, :N_EXPERTS].astype(I32)
    padded = ((cnt + EXPERT_BLOCK - 1) // EXPERT_BLOCK) * EXPERT_BLOCK
    pad_ends = jnp.cumsum(padded)
    pad_starts = pad_ends - padded
    n_blocks = -(-a_total // EXPERT_BLOCK) + N_EXPERTS
    rows_idx = n_blocks + MOE_EXTRA_BLOCKS
    p_slots = rows_idx * EXPERT_BLOCK
    dest = (pad_starts[e_idx] + pos).reshape(-1)
    a_ids = jnp.arange(a_total, dtype=I32)
    slot_a = jnp.full((p_slots,), -1, I32).at[dest].set(a_ids)
    slot_id = jnp.arange(p_slots, dtype=I32)
    slot_tok = jnp.where(slot_a >= 0, slot_a // TOP_K, 0)
    dump = a_total + slot_id % (ROW_BUFS * EXPERT_BLOCK)
    slot_dst = jnp.where(slot_a >= 0, (slot_a % TOP_K) * n + slot_a // TOP_K, dump)
    slot_idx = jnp.concatenate([slot_tok.reshape(rows_idx, EXPERT_BLOCK),
                                slot_dst.reshape(rows_idx, EXPERT_BLOCK)], axis=1) * ROW_TILES
    block_start = jnp.arange(n_blocks, dtype=I32) * EXPERT_BLOCK
    block_e = jnp.minimum(jnp.sum(pad_ends[None, :] <= block_start[:, None], axis=1),
                          N_EXPERTS - 1).astype(I32)

    y_flat = _moe_ffn(block_e, slot_idx, hn, w1g, w1l, b1g, b1l, w2, b2, a_total)
    out = _combine(h, y_flat, meta, row(final_norm_w), tm_merge)
    return out.reshape(batch, seq, d)
```

```python
import functools

import jax
import jax.numpy as jnp
from jax import lax
from jax.experimental import pallas as pl
from jax.experimental.pallas import tpu as pltpu

F32 = jnp.float32
BF16 = jnp.bfloat16
I32 = jnp.int32

D_MODEL = 1024
HEAD_DIM = 64
LANES = 128
RWKV_DIM = 512
RWKV_PAIRS = RWKV_DIM // LANES
DECAY_LORA = 64
AAA_LORA = 64
GATE_LORA = 128
RWKV_PROJ = 3 * RWKV_DIM + DECAY_LORA + AAA_LORA + GATE_LORA
LNX_EPS = 64e-5
SWA_Q_HEADS = 8
SWA_KV_HEADS = 2
SWA_GROUP = SWA_Q_HEADS // SWA_KV_HEADS
SWA_DIM = SWA_Q_HEADS * HEAD_DIM
SWA_KV_DIM = SWA_KV_HEADS * HEAD_DIM
SWA_QP = SWA_Q_HEADS * LANES
SWA_KVP = SWA_KV_HEADS * LANES
SWA_PROJ_P = SWA_QP + 2 * SWA_KVP
Q_BLOCK = 128
ALIBI_MAX = 8.0
MEM_HEADS = 4
MEM_HEAD_DIM = 128
MEM_DIM = MEM_HEADS * MEM_HEAD_DIM
N_EXPERTS = 32
TOP_K = 4
D_FF = 1024
SWIGLU_ALPHA = 1.702
SWIGLU_LIMIT = 7.0
EXPERT_BLOCK = 512
NORM_EPS = 1e-5
CHUNK = 64
NEG_BIG = -1e30

VMEM_LIMIT = 56 * 1024 * 1024


def _cparams(*sem):
    return pltpu.CompilerParams(dimension_semantics=sem, vmem_limit_bytes=VMEM_LIMIT)


def _rms(x, w):
    return x * lax.rsqrt(jnp.mean(x * x, axis=-1, keepdims=True) + NORM_EPS) * w


def _dot(a, b):
    return jnp.dot(a, b, preferred_element_type=F32)


def _dot_nt(a, b):
    return lax.dot_general(a, b, (((1,), (1,)), ((), ())), preferred_element_type=F32)


def _split2(x):
    hi = x.astype(BF16)
    lo = (x - hi.astype(F32)).astype(BF16)
    return hi, lo


def _split3(x):
    hi = x.astype(BF16)
    r1 = x - hi.astype(F32)
    mid = r1.astype(BF16)
    lo = (r1 - mid.astype(F32)).astype(BF16)
    return hi, mid, lo


ROW_TILES = D_MODEL // LANES


def _store_row_tiles(ref, x):
    rows = x.shape[0]
    for c in range(ROW_TILES):
        ref[pl.ds(c, rows, stride=ROW_TILES), :] = x[:, c * LANES:(c + 1) * LANES]


def _load_row_tiles(ref, rows):
    return jnp.concatenate(
        [ref[pl.ds(c, rows, stride=ROW_TILES), :] for c in range(ROW_TILES)], axis=1)


def _group_ones(n, group):
    r = lax.broadcasted_iota(I32, (n, n), 0) // group
    c = lax.broadcasted_iota(I32, (n, n), 1) // group
    return (r == c).astype(BF16)


def _group_sum(x, ones):
    hi, lo = _split2(x)
    return _dot(hi, ones) + _dot(lo, ones)


def _inproj_kernel(x_ref, nw_ref, wr_ref, ws_ref, wq_ref, bs_ref, pr_ref, ps_ref, qm_ref):
    xb = _rms(x_ref[...], nw_ref[...]).astype(BF16)
    pr_ref[...] = _dot(xb, wr_ref[...])
    ps_ref[...] = (_dot(xb, ws_ref[...]) + bs_ref[...]).astype(BF16)
    qm_ref[...] = _dot(xb, wq_ref[...]).astype(BF16)


def _inproj(x2, nw, w_rw, w_sw, w_qm, b_sw, tm):
    n = x2.shape[0]
    const = lambda i: (0, 0)
    row = lambda i: (i, 0)
    return pl.pallas_call(
        _inproj_kernel,
        grid=(n // tm,),
        in_specs=[
            pl.BlockSpec((tm, D_MODEL), row),
            pl.BlockSpec((1, D_MODEL), const),
            pl.BlockSpec((D_MODEL, RWKV_PROJ), const),
            pl.BlockSpec((D_MODEL, SWA_PROJ_P), const),
            pl.BlockSpec((D_MODEL, MEM_DIM), const),
            pl.BlockSpec((1, SWA_PROJ_P), const),
        ],
        out_specs=[
            pl.BlockSpec((tm, RWKV_PROJ), row),
            pl.BlockSpec((tm, SWA_PROJ_P), row),
            pl.BlockSpec((tm, MEM_DIM), row),
        ],
        out_shape=[
            jax.ShapeDtypeStruct((n, RWKV_PROJ), F32),
            jax.ShapeDtypeStruct((n, SWA_PROJ_P), BF16),
            jax.ShapeDtypeStruct((n, MEM_DIM), BF16),
        ],
        compiler_params=_cparams("parallel"),
        name="inproj",
    )(x2, nw, w_rw, w_sw, w_qm, b_sw)


def _memkv_kernel(m_ref, nw_ref, w_ref, o_ref):
    mb = _rms(m_ref[...], nw_ref[...]).astype(BF16)
    o_ref[...] = _dot(mb, w_ref[...]).astype(BF16)


def _memkv(mem2, nw, w, tm):
    n = mem2.shape[0]
    return pl.pallas_call(
        _memkv_kernel,
        grid=(n // tm,),
        in_specs=[
            pl.BlockSpec((tm, D_MODEL), lambda i: (i, 0)),
            pl.BlockSpec((1, D_MODEL), lambda i: (0, 0)),
            pl.BlockSpec((D_MODEL, 2 * MEM_DIM), lambda i: (0, 0)),
        ],
        out_specs=pl.BlockSpec((tm, 2 * MEM_DIM), lambda i: (i, 0)),
        out_shape=jax.ShapeDtypeStruct((n, 2 * MEM_DIM), BF16),
        compiler_params=_cparams("parallel"),
        name="memkv",
    )(mem2, nw, w)


def _rwkv_prep_kernel(p_ref, halo_ref, mu_ref, w0_ref, wwa_ref, a0_ref, g2_ref, kk_ref, ka_ref,
                      rk_ref, r_out, lw_out, k_out, v_out, kk_out, b_out, g_out, bonus_out,
                      *, tiles_per_seq):
    i = pl.program_id(0)
    p = p_ref[...]
    tm = p.shape[0]
    halo = jnp.where(i % tiles_per_seq == 0, 0.0, halo_ref[7:8, :])
    row = lax.broadcasted_iota(I32, p.shape, 0)
    prev = jnp.where(row == 0, halo, pltpu.roll(p, 1, axis=0))
    ps = p + (prev - p) * mu_ref[...]
    r = ps[:, 0:RWKV_DIM]
    k = ps[:, RWKV_DIM:2 * RWKV_DIM]
    v = ps[:, 2 * RWKV_DIM:3 * RWKV_DIM]
    wa = ps[:, 3 * RWKV_DIM:3 * RWKV_DIM + LANES]
    pg = ps[:, 3 * RWKV_DIM + LANES:]
    lane = lax.broadcasted_iota(I32, (tm, LANES), 1)
    z = jnp.where(lane < DECAY_LORA, jnp.tanh(wa), wa).astype(BF16)
    lora = _dot(z, wwa_ref[...])
    w_log = -jax.nn.softplus(-(w0_ref[...] + lora[:, :RWKV_DIM])) - 0.5
    lw_out[...] = -jnp.exp(w_log)
    alr = jax.nn.sigmoid(a0_ref[...] + lora[:, RWKV_DIM:])
    g_out[...] = _dot(jax.nn.sigmoid(pg).astype(BF16), g2_ref[...])
    ones = _group_ones(RWKV_DIM, HEAD_DIM)
    kk = k * kk_ref[...]
    kk = kk / jnp.maximum(jnp.sqrt(_group_sum(kk * kk, ones)), 1e-12)
    k2 = k * (1.0 + (alr - 1.0) * ka_ref[...])
    r_out[...] = r
    k_out[...] = k2
    v_out[...] = v
    kk_out[...] = kk
    b_out[...] = kk * alr
    bonus_out[...] = _group_sum(r * k2 * rk_ref[...], ones) * v


def _rwkv_prep(p_rw, mu, w0, wwa, a0, g2, k_k, k_a, r_k, tm, seq):
    n = p_rw.shape[0]
    tiles_per_seq = seq // tm
    const = lambda i: (0, 0)
    row = lambda i: (i, 0)
    vec = pl.BlockSpec((1, RWKV_DIM), const)
    out = pl.BlockSpec((tm, RWKV_DIM), row)
    return pl.pallas_call(
        functools.partial(_rwkv_prep_kernel, tiles_per_seq=tiles_per_seq),
        grid=(n // tm,),
        in_specs=[
            pl.BlockSpec((tm, RWKV_PROJ), row),
            pl.BlockSpec((8, RWKV_PROJ), lambda i: (jnp.maximum(i * (tm // 8) - 1, 0), 0)),
            pl.BlockSpec((1, RWKV_PROJ), const),
            vec,
            pl.BlockSpec((LANES, 2 * RWKV_DIM), const),
            vec,
            pl.BlockSpec((GATE_LORA, RWKV_DIM), const),
            vec, vec, vec,
        ],
        out_specs=[out] * 8,
        out_shape=[jax.ShapeDtypeStruct((n, RWKV_DIM), F32)] * 8,
        compiler_params=_cparams("parallel"),
        name="rwkv_prep",
    )(p_rw, p_rw, mu, w0, wwa, a0, g2, k_k, k_a, r_k)


def _blockdiag(x, first_head):
    zero = jnp.zeros_like(x)
    return jnp.concatenate([jnp.where(first_head, x, zero), jnp.where(first_head, zero, x)], axis=0)


def _rwkv_scan_kernel(r_ref, lw_ref, k_ref, v_ref, kk_ref, b_ref, g_ref, bonus_ref, lnw_ref,
                      lnb_ref, y_ref, s_ref, *, chunks):
    @pl.when(pl.program_id(1) == 0)
    def _():
        s_ref[...] = jnp.zeros_like(s_ref)

    c2 = 2 * CHUNK
    tri_incl = (lax.broadcasted_iota(I32, (CHUNK, CHUNK), 1)
                <= lax.broadcasted_iota(I32, (CHUNK, CHUNK), 0)).astype(BF16)
    rowi = lax.broadcasted_iota(I32, (c2, c2), 0)
    coli = lax.broadcasted_iota(I32, (c2, c2), 1)
    strict = coli < rowi
    incl = coli <= rowi
    eye = (coli == rowi).astype(F32)
    first_head = lax.broadcasted_iota(I32, (CHUNK, LANES), 1) < HEAD_DIM
    ones = _group_ones(RWKV_DIM, HEAD_DIM)
    inv_n = 1.0 / HEAD_DIM

    seqs = r_ref.shape[0]
    pairs = range(RWKV_PAIRS)
    lanes = [slice(j * LANES, (j + 1) * LANES) for j in pairs]
    units = [(q, j) for q in range(seqs) for j in pairs]

    def chunk_body(c, carry):
        rows = pl.ds(pl.multiple_of(c * CHUNK, CHUNK), CHUNK)
        a_t, r_t, b_t, k_t, b_h, k_h, p_end, v = [], [], [], [], [], [], [], []
        for q in range(seqs):
            lw = lw_ref[q, rows, :]
            hi, mid, lo = _split3(lw)
            cl = _dot(tri_incl, hi) + _dot(tri_incl, mid) + _dot(tri_incl, lo)
            cl_end = cl[CHUNK - 1:CHUNK, :]
            kq = k_ref[q, rows, :]
            bq = b_ref[q, rows, :]
            inv = jnp.exp(-cl)
            to_end = jnp.exp(cl_end - cl)
            a_t.append(-kk_ref[q, rows, :] * jnp.exp(cl - lw))
            r_t.append(r_ref[q, rows, :] * jnp.exp(cl))
            b_t.append(bq * inv)
            k_t.append(kq * inv)
            b_h.append(bq * to_end)
            k_h.append(kq * to_end)
            p_end.append(jnp.exp(cl_end))
            v.append(v_ref[q, rows, :])
        bd = lambda x, u: _blockdiag(x[u[0]][:, lanes[u[1]]], first_head).astype(BF16)
        lhs = [jnp.concatenate([bd(a_t, u), bd(r_t, u)], axis=0) for u in units]
        a4 = [_dot_nt(lhs[i], jnp.concatenate([bd(b_t, u), bd(k_t, u)], axis=0))
              for i, u in enumerate(units)]
        ids = range(len(units))
        a_ab = [jnp.where(strict, a4[i][:c2, :c2], 0.0) for i in ids]
        tinv = [eye + a_ab[i] for i in ids]
        ab = [a_ab[i].astype(BF16) for i in ids]
        apow = [_dot(ab[i], ab[i]) for i in ids]
        for _ in range(4):
            st = [_dot(jnp.concatenate([tinv[i], apow[i]], axis=0).astype(BF16),
                       apow[i].astype(BF16)) for i in ids]
            tinv = [tinv[i] + st[i][:c2] for i in ids]
            apow = [st[i][c2:] for i in ids]
        tinv = [(tinv[i] + _dot(tinv[i].astype(BF16), apow[i].astype(BF16))).astype(BF16)
                for i in ids]
        v2 = [bd(v, u) for u in units]
        a_ak = [jnp.where(strict, a4[i][:c2, c2:], 0.0).astype(BF16) for i in ids]
        a_r = [jnp.concatenate([jnp.where(incl, a4[i][c2:, :c2], 0.0),
                                jnp.where(incl, a4[i][c2:, c2:], 0.0)], axis=1).astype(BF16)
               for i in ids]
        akv = [_dot(a_ak[i], v2[i]) for i in ids]
        s_old = [s_ref[i] for i in ids]
        su = [_dot_nt(lhs[i], s_old[i].astype(BF16)) for i in ids]
        sa = [_dot(tinv[i], (su[i][:c2] + akv[i]).astype(BF16)) for i in ids]
        y2 = [su[i][c2:] + _dot(a_r[i], jnp.concatenate([sa[i].astype(BF16), v2[i]], axis=0))
              for i in ids]
        sav_t = [jnp.concatenate([sa[i], v2[i].astype(F32)], axis=0).T.astype(BF16) for i in ids]
        for i, u in enumerate(units):
            s_ref[i] = s_old[i] * p_end[u[0]][:, lanes[u[1]]] + _dot(
                sav_t[i], jnp.concatenate([bd(b_h, u), bd(k_h, u)], axis=0))
        for q in range(seqs):
            y = jnp.concatenate([y2[q * RWKV_PAIRS + j][:CHUNK] + y2[q * RWKV_PAIRS + j][CHUNK:]
                                 for j in pairs], axis=1)
            m = _group_sum(y, ones) * inv_n
            d = y - m
            var = _group_sum(d * d, ones) * inv_n
            yn = d * lax.rsqrt(var + LNX_EPS) * lnw_ref[...] + lnb_ref[...]
            y_ref[q, rows, :] = ((yn + bonus_ref[q, rows, :]) * g_ref[q, rows, :]).astype(BF16)
        return carry

    lax.fori_loop(0, chunks, chunk_body, 0)


def _rwkv_scan(r, lw, k, v, kk, b, g, bonus, lnw, lnb, batch, seq, tc, seqs):
    as3 = lambda a: a.reshape(batch, seq, RWKV_DIM)
    blk = pl.BlockSpec((seqs, tc, RWKV_DIM), lambda bi, ti: (bi, ti, 0))
    vec = pl.BlockSpec((1, RWKV_DIM), lambda bi, ti: (0, 0))
    y = pl.pallas_call(
        functools.partial(_rwkv_scan_kernel, chunks=tc // CHUNK),
        grid=(batch // seqs, seq // tc),
        in_specs=[blk] * 8 + [vec, vec],
        out_specs=blk,
        out_shape=jax.ShapeDtypeStruct((batch, seq, RWKV_DIM), BF16),
        scratch_shapes=[pltpu.VMEM((seqs * RWKV_PAIRS, LANES, LANES), F32)],
        compiler_params=_cparams("parallel", "arbitrary"),
        name="rwkv_scan",
    )(*(as3(a) for a in (r, lw, k, v, kk, b, g, bonus)), lnw, lnb)
    return y.reshape(batch * seq, RWKV_DIM)


def _swa_kernel(sink_ref, q_ref, kvc_ref, kvp_ref, o_ref):
    nblk = pl.program_id(1)
    kv = jnp.concatenate([kvp_ref[...], kvc_ref[...]], axis=0)
    rows = SWA_GROUP * Q_BLOCK
    row = lax.broadcasted_iota(I32, (rows, 2 * Q_BLOCK), 0)
    gi_of_row = row // Q_BLOCK
    i = row % Q_BLOCK
    j = lax.broadcasted_iota(I32, (rows, 2 * Q_BLOCK), 1)
    dist = i + Q_BLOCK - j
    valid = (dist >= 0) & (dist < Q_BLOCK) & ((j >= Q_BLOCK) | (nblk > 0))
    distf = dist.astype(F32)
    gi_col = lax.broadcasted_iota(I32, (rows, 1), 0) // Q_BLOCK
    for h in range(SWA_KV_HEADS):
        kh = kv[:, h * LANES:(h + 1) * LANES]
        vh = kv[:, SWA_KVP + h * LANES:SWA_KVP + (h + 1) * LANES]
        heads = [h * SWA_GROUP + gi for gi in range(SWA_GROUP)]
        q4 = jnp.concatenate([q_ref[:, hq * LANES:(hq + 1) * LANES] for hq in heads], axis=0)
        bias = jnp.zeros((rows, 2 * Q_BLOCK), F32)
        sink = jnp.zeros((rows, 1), F32)
        for gi, hq in enumerate(heads):
            slope = 2.0 ** (-ALIBI_MAX * (hq + 1) / SWA_Q_HEADS)
            bias = jnp.where(gi_of_row == gi, slope * distf, bias)
            sink = jnp.where(gi_col == gi, sink_ref[hq], sink)
        s = jnp.where(valid, _dot_nt(q4, kh) - bias, NEG_BIG)
        m = jnp.maximum(jnp.max(s, axis=-1, keepdims=True), sink)
        e = jnp.exp(s - m)
        den = jnp.sum(e, axis=-1, keepdims=True) + jnp.exp(sink - m)
        o4 = _dot((e / den).astype(BF16), vh).astype(BF16)
        for gi, hq in enumerate(heads):
            o_ref[:, hq * LANES:(hq + 1) * LANES] = o4[gi * Q_BLOCK:(gi + 1) * Q_BLOCK]


def _swa(p_sw, sinks, batch, seq):
    n = p_sw.shape[0]
    nb = seq // Q_BLOCK
    kv_col = SWA_QP // (2 * SWA_KVP)
    return pl.pallas_call(
        _swa_kernel,
        grid=(batch, nb),
        in_specs=[
            pl.BlockSpec(memory_space=pltpu.SMEM),
            pl.BlockSpec((Q_BLOCK, SWA_QP), lambda bi, ni: (bi * nb + ni, 0)),
            pl.BlockSpec((Q_BLOCK, 2 * SWA_KVP), lambda bi, ni: (bi * nb + ni, kv_col)),
            pl.BlockSpec((Q_BLOCK, 2 * SWA_KVP),
                         lambda bi, ni: (bi * nb + jnp.maximum(ni - 1, 0), kv_col)),
        ],
        out_specs=pl.BlockSpec((Q_BLOCK, SWA_QP), lambda bi, ni: (bi * nb + ni, 0)),
        out_shape=jax.ShapeDtypeStruct((n, SWA_QP), BF16),
        compiler_params=_cparams("parallel", "parallel"),
        name="swa",
    )(sinks, p_sw, p_sw, p_sw)


def _merge_kernel(x_ref, yr_ref, ys_ref, qm_ref, km_ref, vm_ref, anw_ref, wg_ref, wb0_ref, wb1_ref,
                  wb2_ref, wo_ref, fnw_ref, wrh_ref, wrl_ref, br_ref,
                  h_ref, hn_ref, meta_ref, cnt_ref, carry_ref):
    @pl.when(pl.program_id(0) == 0)
    def _():
        carry_ref[...] = jnp.zeros_like(carry_ref)

    x = x_ref[...]
    tm = x.shape[0]
    xb = _rms(x, anw_ref[...]).astype(BF16)
    outs = []
    for hh in range(MEM_HEADS):
        cols = slice(hh * MEM_HEAD_DIM, (hh + 1) * MEM_HEAD_DIM)
        s = _dot_nt(qm_ref[:, cols], km_ref[:, cols]) * (MEM_HEAD_DIM ** -0.5)
        m = jnp.max(s, axis=-1, keepdims=True)
        e = jnp.exp(s - m)
        pr = (e / jnp.sum(e, axis=-1, keepdims=True)).astype(BF16)
        outs.append(_dot(pr, vm_ref[:, cols]))
    y_mem = jnp.concatenate(outs, axis=1).astype(BF16)
    merged = jax.nn.sigmoid(_dot(xb, wg_ref[0])) * _dot(yr_ref[...], wb0_ref[...])
    merged += jax.nn.sigmoid(_dot(xb, wg_ref[1])) * _dot(ys_ref[...], wb1_ref[...])
    merged += jax.nn.sigmoid(_dot(xb, wg_ref[2])) * _dot(y_mem, wb2_ref[...])
    h = x + _dot(merged.astype(BF16), wo_ref[...])
    h_ref[...] = h
    hn = _rms(h, fnw_ref[...])
    _store_row_tiles(hn_ref, hn)
    hi, lo = _split2(hn)
    logits = _dot(hi, wrh_ref[...]) + _dot(hi, wrl_ref[...]) + _dot(lo, wrh_ref[...]) + br_ref[...]
    lane = lax.broadcasted_iota(I32, (tm, LANES), 1).astype(F32)
    work = logits
    vals, idxs, sels = [], [], []
    for _ in range(TOP_K):
        mk = jnp.max(work, axis=-1, keepdims=True)
        ik = jnp.min(jnp.where(work == mk, lane, float(LANES)), axis=-1, keepdims=True)
        sel = lane == ik
        work = jnp.where(sel, 2.0 * NEG_BIG, work)
        vals.append(mk)
        idxs.append(ik)
        sels.append(sel)
    es = [jnp.exp(vk - vals[0]) for vk in vals]
    den = es[0] + es[1] + es[2] + es[3]
    cnt = jnp.zeros((tm, LANES), F32)
    for sel in sels:
        cnt = cnt + sel.astype(F32)
    tri = (lax.broadcasted_iota(I32, (tm, tm), 1) < lax.broadcasted_iota(I32, (tm, tm), 0)).astype(BF16)
    prefix = _dot(tri, cnt.astype(BF16)) + carry_ref[0:1, :]
    meta = jnp.zeros((tm, LANES), F32)
    for kk in range(TOP_K):
        pos = jnp.sum(jnp.where(sels[kk], prefix, 0.0), axis=-1, keepdims=True)
        meta = jnp.where(lane == kk, idxs[kk], meta)
        meta = jnp.where(lane == TOP_K + kk, es[kk] / den, meta)
        meta = jnp.where(lane == 2 * TOP_K + kk, pos, meta)
    meta_ref[...] = meta
    total = carry_ref[0:1, :] + jnp.sum(cnt, axis=0, keepdims=True)
    carry_ref[...] = jnp.broadcast_to(total, carry_ref.shape)
    cnt_ref[...] = jnp.broadcast_to(total, cnt_ref.shape)


def _merge(x2, y_rw, y_sw, q_mem, memkv, anw, wg, wb0, wb1, wb2, wo, fnw, wrh, wrl, br, tm, seq,
           mem_len):
    n = x2.shape[0]
    tiles_per_seq = seq // tm
    row = lambda i: (i, 0)
    const = lambda i: (0, 0)
    return pl.pallas_call(
        _merge_kernel,
        grid=(n // tm,),
        in_specs=[
            pl.BlockSpec((tm, D_MODEL), row),
            pl.BlockSpec((tm, RWKV_DIM), row),
            pl.BlockSpec((tm, SWA_QP), row),
            pl.BlockSpec((tm, MEM_DIM), row),
            pl.BlockSpec((mem_len, MEM_DIM), lambda i: (i // tiles_per_seq, 0)),
            pl.BlockSpec((mem_len, MEM_DIM), lambda i: (i // tiles_per_seq, 1)),
            pl.BlockSpec((1, D_MODEL), const),
            pl.BlockSpec((3, D_MODEL, D_MODEL), lambda i: (0, 0, 0), pipeline_mode=pl.Buffered(1)),
            pl.BlockSpec((RWKV_DIM, D_MODEL), const, pipeline_mode=pl.Buffered(1)),
            pl.BlockSpec((SWA_QP, D_MODEL), const, pipeline_mode=pl.Buffered(1)),
            pl.BlockSpec((MEM_DIM, D_MODEL), const, pipeline_mode=pl.Buffered(1)),
            pl.BlockSpec((D_MODEL, D_MODEL), const, pipeline_mode=pl.Buffered(1)),
            pl.BlockSpec((1, D_MODEL), const),
            pl.BlockSpec((D_MODEL, LANES), const),
            pl.BlockSpec((D_MODEL, LANES), const),
            pl.BlockSpec((1, LANES), const),
        ],
        out_specs=[
            pl.BlockSpec((tm, D_MODEL), row),
            pl.BlockSpec((tm * ROW_TILES, LANES), row),
            pl.BlockSpec((tm, LANES), row),
            pl.BlockSpec((8, LANES), const),
        ],
        out_shape=[
            jax.ShapeDtypeStruct((n, D_MODEL), F32),
            jax.ShapeDtypeStruct((n * ROW_TILES, LANES), F32),
            jax.ShapeDtypeStruct((n, LANES), F32),
            jax.ShapeDtypeStruct((8, LANES), F32),
        ],
        scratch_shapes=[pltpu.VMEM((8, LANES), F32)],
        compiler_params=_cparams("arbitrary"),
        name="merge_router",
    )(x2, y_rw, y_sw, q_mem, memkv, memkv, anw, wg, wb0, wb1, wb2, wo, fnw, wrh, wrl, br)


def _w1_split_kernel(w_ref, g_ref, l_ref, t_ref):
    t_ref[...] = w_ref[...].T
    g_ref[...] = t_ref[pl.ds(0, D_FF, stride=2), :].T.astype(BF16)
    l_ref[...] = t_ref[pl.ds(1, D_FF, stride=2), :].T.astype(BF16)


def _w1_split(w1, rows):
    n_e = w1.shape[0]
    out = pl.BlockSpec((None, rows, D_FF), lambda e, r: (e, r, 0))
    return pl.pallas_call(
        _w1_split_kernel,
        grid=(n_e, D_MODEL // rows),
        in_specs=[pl.BlockSpec((None, rows, 2 * D_FF), lambda e, r: (e, r, 0))],
        out_specs=[out, out],
        out_shape=[jax.ShapeDtypeStruct((n_e, D_MODEL, D_FF), BF16)] * 2,
        scratch_shapes=[pltpu.VMEM((2 * D_FF, rows), F32)],
        compiler_params=_cparams("parallel", "parallel"),
        name="w1_split",
    )(w1)


IDX_RING = 4
ROW_BUFS = 2
MOE_EXTRA_BLOCKS = 2


def _moe_kernel(be_ref, idx_hbm, hn_hbm, w1g_ref, w1l_ref, b1g_ref, b1l_ref, w2_ref, b2_ref, y_hbm,
                idx_smem, xbuf, ybuf, gsem, ssem, isem):
    i = pl.program_id(0)
    last = pl.num_programs(0) - 1
    n_dump = y_hbm.shape[0] - 2 * EXPERT_BLOCK * ROW_TILES

    def tile_rows(start):
        return pl.ds(pl.multiple_of(start, ROW_TILES), ROW_TILES)

    def idx_copy(blk):
        ring = blk % IDX_RING
        return pltpu.make_async_copy(idx_hbm.at[blk], idx_smem.at[ring], isem.at[ring])

    def gather(blk, j):
        slot = blk % ROW_BUFS
        src = idx_smem[blk % IDX_RING, j]
        return pltpu.make_async_copy(hn_hbm.at[tile_rows(src), :],
                                     xbuf.at[slot, pl.ds(j * ROW_TILES, ROW_TILES), :], gsem.at[slot])

    def scatter(blk, j, dst=None):
        slot = blk % ROW_BUFS
        if dst is None:
            dst = idx_smem[blk % IDX_RING, EXPERT_BLOCK + j]
        return pltpu.make_async_copy(ybuf.at[slot, pl.ds(j * ROW_TILES, ROW_TILES), :],
                                     y_hbm.at[tile_rows(dst), :], ssem.at[slot])

    @pl.when(i == 0)
    def _():
        ybuf[...] = jnp.zeros_like(ybuf)
        for slot in range(ROW_BUFS):
            for j in range(EXPERT_BLOCK):
                scatter(slot, j, dst=n_dump + (slot * EXPERT_BLOCK + j) * ROW_TILES).start()
        idx_copy(0).start()
        idx_copy(0).wait()
        for j in range(EXPERT_BLOCK):
            gather(0, j).start()
        idx_copy(1).start()

    idx_copy(i + 1).wait()
    for j in range(EXPERT_BLOCK):
        gather(i + 1, j).start(priority=j % 2)
    for j in range(EXPERT_BLOCK):
        scatter(i, j, dst=0).wait()
    idx_copy(i + 2).start()
    for j in range(EXPERT_BLOCK):
        gather(i, j).wait()

    slot = i % ROW_BUFS
    xb = _load_row_tiles(xbuf.at[slot], EXPERT_BLOCK).astype(BF16)
    glu = _dot(xb, w1g_ref[...]) + b1g_ref[...]
    lin = _dot(xb, w1l_ref[...]) + b1l_ref[...]
    glu = jnp.minimum(glu, SWIGLU_LIMIT)
    lin = jnp.clip(lin, -SWIGLU_LIMIT, SWIGLU_LIMIT)
    act = glu * jax.nn.sigmoid(SWIGLU_ALPHA * glu) * (lin + 1.0)
    _store_row_tiles(ybuf.at[slot], _dot(act.astype(BF16), w2_ref[...]) + b2_ref[...])
    for j in range(EXPERT_BLOCK):
        scatter(i, j).start(priority=j % 2)

    @pl.when(i == last)
    def _():
        for blk in (i - 1, i):
            for j in range(EXPERT_BLOCK):
                scatter(blk, j, dst=0).wait()
        for j in range(EXPERT_BLOCK):
            gather(i + 1, j).wait()
        idx_copy(i + 2).wait()


def _moe_ffn(block_e, slot_idx, hn, w1g, w1l, b1g, b1l, w2, b2, n_rows):
    n_blocks = slot_idx.shape[0] - MOE_EXTRA_BLOCKS
    wspec = pl.BlockSpec((None, D_MODEL, D_FF), lambda i, be: (be[i], 0, 0))
    bspec = pl.BlockSpec((None, 1, D_FF), lambda i, be: (be[i], 0, 0))
    grid_spec = pltpu.PrefetchScalarGridSpec(
        num_scalar_prefetch=1,
        grid=(n_blocks,),
        in_specs=[
            pl.BlockSpec(memory_space=pl.ANY),
            pl.BlockSpec(memory_space=pl.ANY),
            wspec, wspec, bspec, bspec,
            pl.BlockSpec((None, D_FF, D_MODEL), lambda i, be: (be[i], 0, 0)),
            pl.BlockSpec((None, 1, D_MODEL), lambda i, be: (be[i], 0, 0)),
        ],
        out_specs=pl.BlockSpec(memory_space=pl.ANY),
        scratch_shapes=[
            pltpu.SMEM((IDX_RING, 2 * EXPERT_BLOCK), I32),
            pltpu.VMEM((ROW_BUFS, EXPERT_BLOCK * ROW_TILES, LANES), F32),
            pltpu.VMEM((ROW_BUFS, EXPERT_BLOCK * ROW_TILES, LANES), F32),
            pltpu.SemaphoreType.DMA((ROW_BUFS,)),
            pltpu.SemaphoreType.DMA((ROW_BUFS,)),
            pltpu.SemaphoreType.DMA((IDX_RING,)),
        ],
    )
    return pl.pallas_call(
        _moe_kernel,
        grid_spec=grid_spec,
        out_shape=jax.ShapeDtypeStruct(((n_rows + 2 * EXPERT_BLOCK) * ROW_TILES, LANES), F32),
        compiler_params=_cparams("arbitrary"),
        name="moe_ffn",
    )(block_e, slot_idx, hn, w1g, w1l, b1g, b1l, w2, b2)


def _combine_kernel(h_ref, y0_ref, y1_ref, y2_ref, y3_ref, meta_ref, fw_ref, o_ref):
    acc = h_ref[...]
    meta = meta_ref[...]
    for kk, y_ref in enumerate((y0_ref, y1_ref, y2_ref, y3_ref)):
        acc = acc + meta[:, TOP_K + kk:TOP_K + kk + 1] * _load_row_tiles(y_ref, acc.shape[0])
    o_ref[...] = _rms(acc, fw_ref[...])


def _combine(h, y_flat, meta, fw, tm):
    n = h.shape[0]
    tiles = n // tm
    row = lambda i: (i, 0)
    y_specs = [pl.BlockSpec((tm * ROW_TILES, LANES),
                            functools.partial(lambda i, kk: (kk * tiles + i, 0), kk=kk))
               for kk in range(TOP_K)]
    return pl.pallas_call(
        _combine_kernel,
        grid=(tiles,),
        in_specs=[pl.BlockSpec((tm, D_MODEL), row)] + y_specs + [
            pl.BlockSpec((tm, LANES), row),
            pl.BlockSpec((1, D_MODEL), lambda i: (0, 0)),
        ],
        out_specs=pl.BlockSpec((tm, D_MODEL), row),
        out_shape=jax.ShapeDtypeStruct((n, D_MODEL), F32),
        compiler_params=_cparams("parallel"),
        name="combine",
    )(h, y_flat, y_flat, y_flat, y_flat, meta, fw)


def _pad_heads_cols(w, heads):
    lead = w.shape[:-1]
    w = w.reshape(lead + (heads, HEAD_DIM))
    w = jnp.pad(w, [(0, 0)] * len(lead) + [(0, 0), (0, LANES - HEAD_DIM)])
    return w.reshape(lead + (heads * LANES,))


def _pick_tile(n, pref):
    t = pref
    while n % t:
        t //= 2
    return t


def kernel(x, mem, attn_norm_w, mem_norm_w, w_in, b_swa, rwkv_mu, rwkv_w0, rwkv_w2, rwkv_a0,
           rwkv_a2, rwkv_g2, rwkv_kk, rwkv_ka, rwkv_rk, rwkv_lnx_w, rwkv_lnx_b, swa_sinks,
           w_mem_kv, w_gate, w_branch, w_out, ffn_norm_w, w_router, b_router, w_exp1, b_exp1,
           w_exp2, b_exp2, final_norm_w):
    batch, seq, d = x.shape
    mem_len = mem.shape[1]
    n = batch * seq
    layer = 0
    x2 = x.reshape(n, d)
    row = lambda a: a.reshape(1, -1)

    w_in_l = w_in[layer]
    w_rw = w_in_l[:, :RWKV_PROJ].astype(BF16)
    sw = w_in_l[:, RWKV_PROJ:RWKV_PROJ + SWA_DIM + 2 * SWA_KV_DIM]
    bs = b_swa[layer]
    scale = HEAD_DIM ** -0.5
    w_sw = jnp.concatenate([
        _pad_heads_cols(sw[:, :SWA_DIM] * scale, SWA_Q_HEADS),
        _pad_heads_cols(sw[:, SWA_DIM:SWA_DIM + SWA_KV_DIM], SWA_KV_HEADS),
        _pad_heads_cols(sw[:, SWA_DIM + SWA_KV_DIM:], SWA_KV_HEADS)], axis=1).astype(BF16)
    b_sw = row(jnp.concatenate([
        _pad_heads_cols(bs[:SWA_DIM] * scale, SWA_Q_HEADS),
        _pad_heads_cols(bs[SWA_DIM:SWA_DIM + SWA_KV_DIM], SWA_KV_HEADS),
        _pad_heads_cols(bs[SWA_DIM + SWA_KV_DIM:], SWA_KV_HEADS)]))
    w_qm = w_in_l[:, RWKV_PROJ + SWA_DIM + 2 * SWA_KV_DIM:].astype(BF16)
    zeros_l = jnp.zeros((DECAY_LORA, RWKV_DIM), F32)
    wwa = jnp.concatenate([
        jnp.concatenate([rwkv_w2[layer], zeros_l], axis=1),
        jnp.concatenate([zeros_l, rwkv_a2[layer]], axis=1)], axis=0).astype(BF16)
    wb = w_branch[layer]
    wb1 = _pad_heads_cols(wb[1].T, SWA_Q_HEADS).T.astype(BF16)
    wr = jnp.pad(w_router[layer], ((0, 0), (0, LANES - N_EXPERTS)))
    wrh = wr.astype(BF16)
    wrl = (wr - wrh.astype(F32)).astype(BF16)
    br = row(jnp.pad(b_router[layer], (0, LANES - N_EXPERTS), constant_values=NEG_BIG))
    w1 = w_exp1[layer]
    b1 = b_exp1[layer]
    w1g, w1l = _w1_split(w1, LANES)
    b1g = b1[:, None, 0::2]
    b1l = b1[:, None, 1::2]
    w2 = w_exp2[layer].astype(BF16)
    b2 = b_exp2[layer][:, None, :]

    tm_proj = _pick_tile(seq, 512)
    p_rw, p_sw, q_mem = _inproj(x2, row(attn_norm_w[layer]), w_rw, w_sw, w_qm, b_sw, tm_proj)
    memkv = _memkv(mem.reshape(batch * mem_len, d), row(mem_norm_w[layer]),
                   w_mem_kv[layer].astype(BF16), mem_len)
    r, lw, k, v, kk, b, g, bonus = _rwkv_prep(
        p_rw, row(rwkv_mu[layer]), row(rwkv_w0[layer]), wwa, row(rwkv_a0[layer]),
        rwkv_g2[layer].astype(BF16), row(rwkv_kk[layer]), row(rwkv_ka[layer]),
        row(rwkv_rk[layer]), tm_proj, seq)
    y_rw = _rwkv_scan(r, lw, k, v, kk, b, g, bonus, row(rwkv_lnx_w[layer]),
                      row(rwkv_lnx_b[layer]), batch, seq, _pick_tile(seq, 256),
                      4 if batch % 4 == 0 else 1)
    y_sw = _swa(p_sw, swa_sinks[layer], batch, seq)

    tm_merge = _pick_tile(seq, 512)
    h, hn, meta, counts = _merge(
        x2, y_rw, y_sw, q_mem, memkv, row(attn_norm_w[layer]), w_gate[layer].astype(BF16),
        wb[0].astype(BF16), wb1, wb[2].astype(BF16), w_out[layer].astype(BF16),
        row(ffn_norm_w[layer]), wrh, wrl, br, tm_merge, seq, mem_len)

    a_total = n * TOP_K
    e_idx = meta[:, 0:TOP_K].astype(I32)
    pos = meta[:, 2 * TOP_K:3 * TOP_K].astype(I32)
    cnt = counts[0, :N_EXPERTS].astype(I32)
    padded = ((cnt + EXPERT_BLOCK - 1) // EXPERT_BLOCK) * EXPERT_BLOCK
    pad_ends = jnp.cumsum(padded)
    pad_starts = pad_ends - padded
    n_blocks = -(-a_total // EXPERT_BLOCK) + N_EXPERTS
    rows_idx = n_blocks + MOE_EXTRA_BLOCKS
    p_slots = rows_idx * EXPERT_BLOCK
    dest = (pad_starts[e_idx] + pos).reshape(-1)
    a_ids = jnp.arange(a_total, dtype=I32)
    slot_a = jnp.full((p_slots,), -1, I32).at[dest].set(a_ids)
    slot_id = jnp.arange(p_slots, dtype=I32)
    slot_tok = jnp.where(slot_a >= 0, slot_a // TOP_K, 0)
    dump = a_total + slot_id % (ROW_BUFS * EXPERT_BLOCK)
    slot_dst = jnp.where(slot_a >= 0, (slot_a % TOP_K) * n + slot_a // TOP_K, dump)
    slot_idx = jnp.concatenate([slot_tok.reshape(rows_idx, EXPERT_BLOCK),
                                slot_dst.reshape(rows_idx, EXPERT_BLOCK)], axis=1) * ROW_TILES
    block_start = jnp.arange(n_blocks, dtype=I32) * EXPERT_BLOCK
    block_e = jnp.minimum(jnp.sum(pad_ends[None, :] <= block_start[:, None], axis=1),
                          N_EXPERTS - 1).astype(I32)

    y_flat = _moe_ffn(block_e, slot_idx, hn, w1g, w1l, b1g, b1l, w2, b2, a_total)
    out = _combine(h, y_flat, meta, row(final_norm_w), tm_merge)
    return out.reshape(batch, seq, d)
```

```python
import functools

import jax
import jax.numpy as jnp
from jax import lax
from jax.experimental import pallas as pl
from jax.experimental.pallas import tpu as pltpu

F32 = jnp.float32
BF16 = jnp.bfloat16
I32 = jnp.int32

D_MODEL = 1024
HEAD_DIM = 64
LANES = 128
RWKV_DIM = 512
RWKV_PAIRS = RWKV_DIM // LANES
DECAY_LORA = 64
AAA_LORA = 64
GATE_LORA = 128
RWKV_PROJ = 3 * RWKV_DIM + DECAY_LORA + AAA_LORA + GATE_LORA
LNX_EPS = 64e-5
SWA_Q_HEADS = 8
SWA_KV_HEADS = 2
SWA_GROUP = SWA_Q_HEADS // SWA_KV_HEADS
SWA_DIM = SWA_Q_HEADS * HEAD_DIM
SWA_KV_DIM = SWA_KV_HEADS * HEAD_DIM
SWA_QP = SWA_Q_HEADS * LANES
SWA_KVP = SWA_KV_HEADS * LANES
SWA_PROJ_P = SWA_QP + 2 * SWA_KVP
Q_BLOCK = 128
ALIBI_MAX = 8.0
MEM_HEADS = 4
MEM_HEAD_DIM = 128
MEM_DIM = MEM_HEADS * MEM_HEAD_DIM
N_EXPERTS = 32
TOP_K = 4
D_FF = 1024
SWIGLU_ALPHA = 1.702
SWIGLU_LIMIT = 7.0
EXPERT_BLOCK = 512
NORM_EPS = 1e-5
CHUNK = 64
NEG_BIG = -1e30

VMEM_LIMIT = 56 * 1024 * 1024


def _cparams(*sem):
    return pltpu.CompilerParams(dimension_semantics=sem, vmem_limit_bytes=VMEM_LIMIT)


def _rms(x, w):
    return x * lax.rsqrt(jnp.mean(x * x, axis=-1, keepdims=True) + NORM_EPS) * w


def _dot(a, b):
    return jnp.dot(a, b, preferred_element_type=F32)


def _dot_nt(a, b):
    return lax.dot_general(a, b, (((1,), (1,)), ((), ())), preferred_element_type=F32)


def _split2(x):
    hi = x.astype(BF16)
    lo = (x - hi.astype(F32)).astype(BF16)
    return hi, lo


def _split3(x):
    hi = x.astype(BF16)
    r1 = x - hi.astype(F32)
    mid = r1.astype(BF16)
    lo = (r1 - mid.astype(F32)).astype(BF16)
    return hi, mid, lo


ROW_TILES = D_MODEL // LANES


def _store_row_tiles(ref, x):
    rows = x.shape[0]
    for c in range(ROW_TILES):
        ref[pl.ds(c, rows, stride=ROW_TILES), :] = x[:, c * LANES:(c + 1) * LANES]


def _load_row_tiles(ref, rows):
    return jnp.concatenate(
        [ref[pl.ds(c, rows, stride=ROW_TILES), :] for c in range(ROW_TILES)], axis=1)


def _group_ones(n, group):
    r = lax.broadcasted_iota(I32, (n, n), 0) // group
    c = lax.broadcasted_iota(I32, (n, n), 1) // group
    return (r == c).astype(BF16)


def _group_sum(x, ones):
    hi, lo = _split2(x)
    return _dot(hi, ones) + _dot(lo, ones)


def _inproj_kernel(x_ref, nw_ref, wr_ref, ws_ref, wq_ref, bs_ref, pr_ref, ps_ref, qm_ref):
    xb = _rms(x_ref[...], nw_ref[...]).astype(BF16)
    pr_ref[...] = _dot(xb, wr_ref[...])
    ps_ref[...] = (_dot(xb, ws_ref[...]) + bs_ref[...]).astype(BF16)
    qm_ref[...] = _dot(xb, wq_ref[...]).astype(BF16)


def _inproj(x2, nw, w_rw, w_sw, w_qm, b_sw, tm):
    n = x2.shape[0]
    const = lambda i: (0, 0)
    row = lambda i: (i, 0)
    return pl.pallas_call(
        _inproj_kernel,
        grid=(n // tm,),
        in_specs=[
            pl.BlockSpec((tm, D_MODEL), row),
            pl.BlockSpec((1, D_MODEL), const),
            pl.BlockSpec((D_MODEL, RWKV_PROJ), const),
            pl.BlockSpec((D_MODEL, SWA_PROJ_P), const),
            pl.BlockSpec((D_MODEL, MEM_DIM), const),
            pl.BlockSpec((1, SWA_PROJ_P), const),
        ],
        out_specs=[
            pl.BlockSpec((tm, RWKV_PROJ), row),
            pl.BlockSpec((tm, SWA_PROJ_P), row),
            pl.BlockSpec((tm, MEM_DIM), row),
        ],
        out_shape=[
            jax.ShapeDtypeStruct((n, RWKV_PROJ), F32),
            jax.ShapeDtypeStruct((n, SWA_PROJ_P), BF16),
            jax.ShapeDtypeStruct((n, MEM_DIM), BF16),
        ],
        compiler_params=_cparams("parallel"),
        name="inproj",
    )(x2, nw, w_rw, w_sw, w_qm, b_sw)


def _memkv_kernel(m_ref, nw_ref, w_ref, o_ref):
    mb = _rms(m_ref[...], nw_ref[...]).astype(BF16)
    o_ref[...] = _dot(mb, w_ref[...]).astype(BF16)


def _memkv(mem2, nw, w, tm):
    n = mem2.shape[0]
    return pl.pallas_call(
        _memkv_kernel,
        grid=(n // tm,),
        in_specs=[
            pl.BlockSpec((tm, D_MODEL), lambda i: (i, 0)),
            pl.BlockSpec((1, D_MODEL), lambda i: (0, 0)),
            pl.BlockSpec((D_MODEL, 2 * MEM_DIM), lambda i: (0, 0)),
        ],
        out_specs=pl.BlockSpec((tm, 2 * MEM_DIM), lambda i: (i, 0)),
        out_shape=jax.ShapeDtypeStruct((n, 2 * MEM_DIM), BF16),
        compiler_params=_cparams("parallel"),
        name="memkv",
    )(mem2, nw, w)


def _rwkv_prep_kernel(p_ref, halo_ref, mu_ref, w0_ref, wwa_ref, a0_ref, g2_ref, kk_ref, ka_ref,
                      rk_ref, r_out, lw_out, k_out, v_out, kk_out, b_out, g_out, bonus_out,
                      *, tiles_per_seq):
    i = pl.program_id(0)
    p = p_ref[...]
    tm = p.shape[0]
    halo = jnp.where(i % tiles_per_seq == 0, 0.0, halo_ref[7:8, :])
    row = lax.broadcasted_iota(I32, p.shape, 0)
    prev = jnp.where(row == 0, halo, pltpu.roll(p, 1, axis=0))
    ps = p + (prev - p) * mu_ref[...]
    r = ps[:, 0:RWKV_DIM]
    k = ps[:, RWKV_DIM:2 * RWKV_DIM]
    v = ps[:, 2 * RWKV_DIM:3 * RWKV_DIM]
    wa = ps[:, 3 * RWKV_DIM:3 * RWKV_DIM + LANES]
    pg = ps[:, 3 * RWKV_DIM + LANES:]
    lane = lax.broadcasted_iota(I32, (tm, LANES), 1)
    z = jnp.where(lane < DECAY_LORA, jnp.tanh(wa), wa).astype(BF16)
    lora = _dot(z, wwa_ref[...])
    w_log = -jax.nn.softplus(-(w0_ref[...] + lora[:, :RWKV_DIM])) - 0.5
    lw_out[...] = -jnp.exp(w_log)
    alr = jax.nn.sigmoid(a0_ref[...] + lora[:, RWKV_DIM:])
    g_out[...] = _dot(jax.nn.sigmoid(pg).astype(BF16), g2_ref[...])
    ones = _group_ones(RWKV_DIM, HEAD_DIM)
    kk = k * kk_ref[...]
    kk = kk / jnp.maximum(jnp.sqrt(_group_sum(kk * kk, ones)), 1e-12)
    k2 = k * (1.0 + (alr - 1.0) * ka_ref[...])
    r_out[...] = r
    k_out[...] = k2
    v_out[...] = v
    kk_out[...] = kk
    b_out[...] = kk * alr
    bonus_out[...] = _group_sum(r * k2 * rk_ref[...], ones) * v


def _rwkv_prep(p_rw, mu, w0, wwa, a0, g2, k_k, k_a, r_k, tm, seq):
    n = p_rw.shape[0]
    tiles_per_seq = seq // tm
    const = lambda i: (0, 0)
    row = lambda i: (i, 0)
    vec = pl.BlockSpec((1, RWKV_DIM), const)
    out = pl.BlockSpec((tm, RWKV_DIM), row)
    return pl.pallas_call(
        functools.partial(_rwkv_prep_kernel, tiles_per_seq=tiles_per_seq),
        grid=(n // tm,),
        in_specs=[
            pl.BlockSpec((tm, RWKV_PROJ), row),
            pl.BlockSpec((8, RWKV_PROJ), lambda i: (jnp.maximum(i * (tm // 8) - 1, 0), 0)),
            pl.BlockSpec((1, RWKV_PROJ), const),
            vec,
            pl.BlockSpec((LANES, 2 * RWKV_DIM), const),
            vec,
            pl.BlockSpec((GATE_LORA, RWKV_DIM), const),
            vec, vec, vec,
        ],
        out_specs=[out] * 8,
        out_shape=[jax.ShapeDtypeStruct((n, RWKV_DIM), F32)] * 8,
        compiler_params=_cparams("parallel"),
        name="rwkv_prep",
    )(p_rw, p_rw, mu, w0, wwa, a0, g2, k_k, k_a, r_k)


def _blockdiag(x, first_head):
    zero = jnp.zeros_like(x)
    return jnp.concatenate([jnp.where(first_head, x, zero), jnp.where(first_head, zero, x)], axis=0)


def _rwkv_scan_kernel(r_ref, lw_ref, k_ref, v_ref, kk_ref, b_ref, g_ref, bonus_ref, lnw_ref,
                      lnb_ref, y_ref, s_ref, *, chunks):
    @pl.when(pl.program_id(1) == 0)
    def _():
        s_ref[...] = jnp.zeros_like(s_ref)

    c2 = 2 * CHUNK
    tri_incl = (lax.broadcasted_iota(I32, (CHUNK, CHUNK), 1)
                <= lax.broadcasted_iota(I32, (CHUNK, CHUNK), 0)).astype(BF16)
    rowi = lax.broadcasted_iota(I32, (c2, c2), 0)
    coli = lax.broadcasted_iota(I32, (c2, c2), 1)
    strict = coli < rowi
    incl = coli <= rowi
    eye = (coli == rowi).astype(F32)
    first_head = lax.broadcasted_iota(I32, (CHUNK, LANES), 1) < HEAD_DIM
    ones = _group_ones(RWKV_DIM, HEAD_DIM)
    inv_n = 1.0 / HEAD_DIM

    seqs = r_ref.shape[0]
    pairs = range(RWKV_PAIRS)
    lanes = [slice(j * LANES, (j + 1) * LANES) for j in pairs]
    units = [(q, j) for q in range(seqs) for j in pairs]

    def chunk_body(c, carry):
        rows = pl.ds(pl.multiple_of(c * CHUNK, CHUNK), CHUNK)
        a_t, r_t, b_t, k_t, b_h, k_h, p_end, v = [], [], [], [], [], [], [], []
        for q in range(seqs):
            lw = lw_ref[q, rows, :]
            hi, mid, lo = _split3(lw)
            cl = _dot(tri_incl, hi) + _dot(tri_incl, mid) + _dot(tri_incl, lo)
            cl_end = cl[CHUNK - 1:CHUNK, :]
            kq = k_ref[q, rows, :]
            bq = b_ref[q, rows, :]
            inv = jnp.exp(-cl)
            to_end = jnp.exp(cl_end - cl)
            a_t.append(-kk_ref[q, rows, :] * jnp.exp(cl - lw))
            r_t.append(r_ref[q, rows, :] * jnp.exp(cl))
            b_t.append(bq * inv)
            k_t.append(kq * inv)
            b_h.append(bq * to_end)
            k_h.append(kq * to_end)
            p_end.append(jnp.exp(cl_end))
            v.append(v_ref[q, rows, :])
        bd = lambda x, u: _blockdiag(x[u[0]][:, lanes[u[1]]], first_head).astype(BF16)
        lhs = [jnp.concatenate([bd(a_t, u), bd(r_t, u)], axis=0) for u in units]
        a4 = [_dot_nt(lhs[i], jnp.concatenate([bd(b_t, u), bd(k_t, u)], axis=0))
              for i, u in enumerate(units)]
        ids = range(len(units))
        a_ab = [jnp.where(strict, a4[i][:c2, :c2], 0.0) for i in ids]
        tinv = [eye + a_ab[i] for i in ids]
        ab = [a_ab[i].astype(BF16) for i in ids]
        apow = [_dot(ab[i], ab[i]) for i in ids]
        for _ in range(4):
            st = [_dot(jnp.concatenate([tinv[i], apow[i]], axis=0).astype(BF16),
                       apow[i].astype(BF16)) for i in ids]
            tinv = [tinv[i] + st[i][:c2] for i in ids]
            apow = [st[i][c2:] for i in ids]
        tinv = [(tinv[i] + _dot(tinv[i].astype(BF16), apow[i].astype(BF16))).astype(BF16)
                for i in ids]
        v2 = [bd(v, u) for u in units]
        a_ak = [jnp.where(strict, a4[i][:c2, c2:], 0.0).astype(BF16) for i in ids]
        a_r = [jnp.concatenate([jnp.where(incl, a4[i][c2:, :c2], 0.0),
                                jnp.where(incl, a4[i][c2:, c2:], 0.0)], axis=1).astype(BF16)
               for i in ids]
        akv = [_dot(a_ak[i], v2[i]) for i in ids]
        s_old = [s_ref[i] for i in ids]
        su = [_dot_nt(lhs[i], s_old[i].astype(BF16)) for i in ids]
        sa = [_dot(tinv[i], (su[i][:c2] + akv[i]).astype(BF16)) for i in ids]
        y2 = [su[i][c2:] + _dot(a_r[i], jnp.concatenate([sa[i].astype(BF16), v2[i]], axis=0))
              for i in ids]
        sav_t = [jnp.concatenate([sa[i], v2[i].astype(F32)], axis=0).T.astype(BF16) for i in ids]
        for i, u in enumerate(units):
            s_ref[i] = s_old[i] * p_end[u[0]][:, lanes[u[1]]] + _dot(
                sav_t[i], jnp.concatenate([bd(b_h, u), bd(k_h, u)], axis=0))
        for q in range(seqs):
            y = jnp.concatenate([y2[q * RWKV_PAIRS + j][:CHUNK] + y2[q * RWKV_PAIRS + j][CHUNK:]
                                 for j in pairs], axis=1)
            m = _group_sum(y, ones) * inv_n
            d = y - m
            var = _group_sum(d * d, ones) * inv_n
            yn = d * lax.rsqrt(var + LNX_EPS) * lnw_ref[...] + lnb_ref[...]
            y_ref[q, rows, :] = ((yn + bonus_ref[q, rows, :]) * g_ref[q, rows, :]).astype(BF16)
        return carry

    lax.fori_loop(0, chunks, chunk_body, 0)


def _rwkv_scan(r, lw, k, v, kk, b, g, bonus, lnw, lnb, batch, seq, tc, seqs):
    as3 = lambda a: a.reshape(batch, seq, RWKV_DIM)
    blk = pl.BlockSpec((seqs, tc, RWKV_DIM), lambda bi, ti: (bi, ti, 0))
    vec = pl.BlockSpec((1, RWKV_DIM), lambda bi, ti: (0, 0))
    y = pl.pallas_call(
        functools.partial(_rwkv_scan_kernel, chunks=tc // CHUNK),
        grid=(batch // seqs, seq // tc),
        in_specs=[blk] * 8 + [vec, vec],
        out_specs=blk,
        out_shape=jax.ShapeDtypeStruct((batch, seq, RWKV_DIM), BF16),
        scratch_shapes=[pltpu.VMEM((seqs * RWKV_PAIRS, LANES, LANES), F32)],
        compiler_params=_cparams("parallel", "arbitrary"),
        name="rwkv_scan",
    )(*(as3(a) for a in (r, lw, k, v, kk, b, g, bonus)), lnw, lnb)
    return y.reshape(batch * seq, RWKV_DIM)


def _swa_kernel(sink_ref, q_ref, kvc_ref, kvp_ref, o_ref):
    nblk = pl.program_id(1)
    kv = jnp.concatenate([kvp_ref[...], kvc_ref[...]], axis=0)
    rows = SWA_GROUP * Q_BLOCK
    row = lax.broadcasted_iota(I32, (rows, 2 * Q_BLOCK), 0)
    gi_of_row = row // Q_BLOCK
    i = row % Q_BLOCK
    j = lax.broadcasted_iota(I32, (rows, 2 * Q_BLOCK), 1)
    dist = i + Q_BLOCK - j
    valid = (dist >= 0) & (dist < Q_BLOCK) & ((j >= Q_BLOCK) | (nblk > 0))
    distf = dist.astype(F32)
    gi_col = lax.broadcasted_iota(I32, (rows, 1), 0) // Q_BLOCK
    for h in range(SWA_KV_HEADS):
        kh = kv[:, h * LANES:(h + 1) * LANES]
        vh = kv[:, SWA_KVP + h * LANES:SWA_KVP + (h + 1) * LANES]
        heads = [h * SWA_GROUP + gi for gi in range(SWA_GROUP)]
        q4 = jnp.concatenate([q_ref[:, hq * LANES:(hq + 1) * LANES] for hq in heads], axis=0)
        bias = jnp.zeros((rows, 2 * Q_BLOCK), F32)
        sink = jnp.zeros((rows, 1), F32)
        for gi, hq in enumerate(heads):
            slope = 2.0 ** (-ALIBI_MAX * (hq + 1) / SWA_Q_HEADS)
            bias = jnp.where(gi_of_row == gi, slope * distf, bias)
            sink = jnp.where(gi_col == gi, sink_ref[hq], sink)
        s = jnp.where(valid, _dot_nt(q4, kh) - bias, NEG_BIG)
        m = jnp.maximum(jnp.max(s, axis=-1, keepdims=True), sink)
        e = jnp.exp(s - m)
        den = jnp.sum(e, axis=-1, keepdims=True) + jnp.exp(sink - m)
        o4 = _dot((e / den).astype(BF16), vh).astype(BF16)
        for gi, hq in enumerate(heads):
            o_ref[:, hq * LANES:(hq + 1) * LANES] = o4[gi * Q_BLOCK:(gi + 1) * Q_BLOCK]


def _swa(p_sw, sinks, batch, seq):
    n = p_sw.shape[0]
    nb = seq // Q_BLOCK
    kv_col = SWA_QP // (2 * SWA_KVP)
    return pl.pallas_call(
        _swa_kernel,
        grid=(batch, nb),
        in_specs=[
            pl.BlockSpec(memory_space=pltpu.SMEM),
            pl.BlockSpec((Q_BLOCK, SWA_QP), lambda bi, ni: (bi * nb + ni, 0)),
            pl.BlockSpec((Q_BLOCK, 2 * SWA_KVP), lambda bi, ni: (bi * nb + ni, kv_col)),
            pl.BlockSpec((Q_BLOCK, 2 * SWA_KVP),
                         lambda bi, ni: (bi * nb + jnp.maximum(ni - 1, 0), kv_col)),
        ],
        out_specs=pl.BlockSpec((Q_BLOCK, SWA_QP), lambda bi, ni: (bi * nb + ni, 0)),
        out_shape=jax.ShapeDtypeStruct((n, SWA_QP), BF16),
        compiler_params=_cparams("parallel", "parallel"),
        name="swa",
    )(sinks, p_sw, p_sw, p_sw)


def _merge_kernel(x_ref, yr_ref, ys_ref, qm_ref, km_ref, vm_ref, anw_ref, wg_ref, wb0_ref, wb1_ref,
                  wb2_ref, wo_ref, fnw_ref, wrh_ref, wrl_ref, br_ref,
                  h_ref, hn_ref, meta_ref, cnt_ref, carry_ref):
    @pl.when(pl.program_id(0) == 0)
    def _():
        carry_ref[...] = jnp.zeros_like(carry_ref)

    x = x_ref[...]
    tm = x.shape[0]
    xb = _rms(x, anw_ref[...]).astype(BF16)
    outs = []
    for hh in range(MEM_HEADS):
        cols = slice(hh * MEM_HEAD_DIM, (hh + 1) * MEM_HEAD_DIM)
        s = _dot_nt(qm_ref[:, cols], km_ref[:, cols]) * (MEM_HEAD_DIM ** -0.5)
        m = jnp.max(s, axis=-1, keepdims=True)
        e = jnp.exp(s - m)
        pr = (e / jnp.sum(e, axis=-1, keepdims=True)).astype(BF16)
        outs.append(_dot(pr, vm_ref[:, cols]))
    y_mem = jnp.concatenate(outs, axis=1).astype(BF16)
    merged = jax.nn.sigmoid(_dot(xb, wg_ref[0])) * _dot(yr_ref[...], wb0_ref[...])
    merged += jax.nn.sigmoid(_dot(xb, wg_ref[1])) * _dot(ys_ref[...], wb1_ref[...])
    merged += jax.nn.sigmoid(_dot(xb, wg_ref[2])) * _dot(y_mem, wb2_ref[...])
    h = x + _dot(merged.astype(BF16), wo_ref[...])
    h_ref[...] = h
    hn = _rms(h, fnw_ref[...])
    _store_row_tiles(hn_ref, hn)
    hi, lo = _split2(hn)
    logits = _dot(hi, wrh_ref[...]) + _dot(hi, wrl_ref[...]) + _dot(lo, wrh_ref[...]) + br_ref[...]
    lane = lax.broadcasted_iota(I32, (tm, LANES), 1).astype(F32)
    work = logits
    vals, idxs, sels = [], [], []
    for _ in range(TOP_K):
        mk = jnp.max(work, axis=-1, keepdims=True)
        ik = jnp.min(jnp.where(work == mk, lane, float(LANES)), axis=-1, keepdims=True)
        sel = lane == ik
        work = jnp.where(sel, 2.0 * NEG_BIG, work)
        vals.append(mk)
        idxs.append(ik)
        sels.append(sel)
    es = [jnp.exp(vk - vals[0]) for vk in vals]
    den = es[0] + es[1] + es[2] + es[3]
    cnt = jnp.zeros((tm, LANES), F32)
    for sel in sels:
        cnt = cnt + sel.astype(F32)
    tri = (lax.broadcasted_iota(I32, (tm, tm), 1) < lax.broadcasted_iota(I32, (tm, tm), 0)).astype(BF16)
    prefix = _dot(tri, cnt.astype(BF16)) + carry_ref[0:1, :]
    meta = jnp.zeros((tm, LANES), F32)
    for kk in range(TOP_K):
        pos = jnp.sum(jnp.where(sels[kk], prefix, 0.0), axis=-1, keepdims=True)
        meta = jnp.where(lane == kk, idxs[kk], meta)
        meta = jnp.where(lane == TOP_K + kk, es[kk] / den, meta)
        meta = jnp.where(lane == 2 * TOP_K + kk, pos, meta)
    meta_ref[...] = meta
    total = carry_ref[0:1, :] + jnp.sum(cnt, axis=0, keepdims=True)
    carry_ref[...] = jnp.broadcast_to(total, carry_ref.shape)
    cnt_ref[...] = jnp.broadcast_to(total, cnt_ref.shape)


def _merge(x2, y_rw, y_sw, q_mem, memkv, anw, wg, wb0, wb1, wb2, wo, fnw, wrh, wrl, br, tm, seq,
           mem_len):
    n = x2.shape[0]
    tiles_per_seq = seq // tm
    row = lambda i: (i, 0)
    const = lambda i: (0, 0)
    return pl.pallas_call(
        _merge_kernel,
        grid=(n // tm,),
        in_specs=[
            pl.BlockSpec((tm, D_MODEL), row),
            pl.BlockSpec((tm, RWKV_DIM), row),
            pl.BlockSpec((tm, SWA_QP), row),
            pl.BlockSpec((tm, MEM_DIM), row),
            pl.BlockSpec((mem_len, MEM_DIM), lambda i: (i // tiles_per_seq, 0)),
            pl.BlockSpec((mem_len, MEM_DIM), lambda i: (i // tiles_per_seq, 1)),
            pl.BlockSpec((1, D_MODEL), const),
            pl.BlockSpec((3, D_MODEL, D_MODEL), lambda i: (0, 0, 0), pipeline_mode=pl.Buffered(1)),
            pl.BlockSpec((RWKV_DIM, D_MODEL), const, pipeline_mode=pl.Buffered(1)),
            pl.BlockSpec((SWA_QP, D_MODEL), const, pipeline_mode=pl.Buffered(1)),
            pl.BlockSpec((MEM_DIM, D_MODEL), const, pipeline_mode=pl.Buffered(1)),
            pl.BlockSpec((D_MODEL, D_MODEL), const, pipeline_mode=pl.Buffered(1)),
            pl.BlockSpec((1, D_MODEL), const),
            pl.BlockSpec((D_MODEL, LANES), const),
            pl.BlockSpec((D_MODEL, LANES), const),
            pl.BlockSpec((1, LANES), const),
        ],
        out_specs=[
            pl.BlockSpec((tm, D_MODEL), row),
            pl.BlockSpec((tm * ROW_TILES, LANES), row),
            pl.BlockSpec((tm, LANES), row),
            pl.BlockSpec((8, LANES), const),
        ],
        out_shape=[
            jax.ShapeDtypeStruct((n, D_MODEL), F32),
            jax.ShapeDtypeStruct((n * ROW_TILES, LANES), F32),
            jax.ShapeDtypeStruct((n, LANES), F32),
            jax.ShapeDtypeStruct((8, LANES), F32),
        ],
        scratch_shapes=[pltpu.VMEM((8, LANES), F32)],
        compiler_params=_cparams("arbitrary"),
        name="merge_router",
    )(x2, y_rw, y_sw, q_mem, memkv, memkv, anw, wg, wb0, wb1, wb2, wo, fnw, wrh, wrl, br)


def _w1_split_kernel(w_ref, g_ref, l_ref, t_ref):
    t_ref[...] = w_ref[...].T
    g_ref[...] = t_ref[pl.ds(0, D_FF, stride=2), :].T.astype(BF16)
    l_ref[...] = t_ref[pl.ds(1, D_FF, stride=2), :].T.astype(BF16)


def _w1_split(w1, rows):
    n_e = w1.shape[0]
    out = pl.BlockSpec((None, rows, D_FF), lambda e, r: (e, r, 0))
    return pl.pallas_call(
        _w1_split_kernel,
        grid=(n_e, D_MODEL // rows),
        in_specs=[pl.BlockSpec((None, rows, 2 * D_FF), lambda e, r: (e, r, 0))],
        out_specs=[out, out],
        out_shape=[jax.ShapeDtypeStruct((n_e, D_MODEL, D_FF), BF16)] * 2,
        scratch_shapes=[pltpu.VMEM((2 * D_FF, rows), F32)],
        compiler_params=_cparams("parallel", "parallel"),
        name="w1_split",
    )(w1)


IDX_RING = 4
ROW_BUFS = 2
MOE_EXTRA_BLOCKS = 2


def _moe_kernel(be_ref, idx_hbm, hn_hbm, w1g_ref, w1l_ref, b1g_ref, b1l_ref, w2_ref, b2_ref, y_hbm,
                idx_smem, xbuf, ybuf, gsem, ssem, isem):
    i = pl.program_id(0)
    last = pl.num_programs(0) - 1
    n_dump = y_hbm.shape[0] - 2 * EXPERT_BLOCK * ROW_TILES

    def tile_rows(start):
        return pl.ds(pl.multiple_of(start, ROW_TILES), ROW_TILES)

    def idx_copy(blk):
        ring = blk % IDX_RING
        return pltpu.make_async_copy(idx_hbm.at[blk], idx_smem.at[ring], isem.at[ring])

    def gather(blk, j):
        slot = blk % ROW_BUFS
        src = idx_smem[blk % IDX_RING, j]
        return pltpu.make_async_copy(hn_hbm.at[tile_rows(src), :], xbuf.at[slot, :, j, :],
                                     gsem.at[slot])

    def scatter(blk, j, dst=None):
        slot = blk % ROW_BUFS
        if dst is None:
            dst = idx_smem[blk % IDX_RING, EXPERT_BLOCK + j]
        return pltpu.make_async_copy(ybuf.at[slot, :, j, :], y_hbm.at[tile_rows(dst), :],
                                     ssem.at[slot])

    @pl.when(i == 0)
    def _():
        ybuf[...] = jnp.zeros_like(ybuf)
        for slot in range(ROW_BUFS):
            for j in range(EXPERT_BLOCK):
                scatter(slot, j, dst=n_dump + (slot * EXPERT_BLOCK + j) * ROW_TILES).start()
        idx_copy(0).start()
        idx_copy(0).wait()
        for j in range(EXPERT_BLOCK):
            gather(0, j).start()
        idx_copy(1).start()

    idx_copy(i + 1).wait()
    for j in range(EXPERT_BLOCK):
        gather(i + 1, j).start(priority=j % 2)
    for j in range(EXPERT_BLOCK):
        scatter(i, j, dst=0).wait()
    idx_copy(i + 2).start()
    for j in range(EXPERT_BLOCK):
        gather(i, j).wait()

    slot = i % ROW_BUFS
    xb = jnp.concatenate([xbuf[slot, c] for c in range(ROW_TILES)], axis=1).astype(BF16)
    glu = _dot(xb, w1g_ref[...]) + b1g_ref[...]
    lin = _dot(xb, w1l_ref[...]) + b1l_ref[...]
    glu = jnp.minimum(glu, SWIGLU_LIMIT)
    lin = jnp.clip(lin, -SWIGLU_LIMIT, SWIGLU_LIMIT)
    act = glu * jax.nn.sigmoid(SWIGLU_ALPHA * glu) * (lin + 1.0)
    yv = _dot(act.astype(BF16), w2_ref[...]) + b2_ref[...]
    for c in range(ROW_TILES):
        ybuf[slot, c] = yv[:, c * LANES:(c + 1) * LANES]
    for j in range(EXPERT_BLOCK):
        scatter(i, j).start(priority=j % 2)

    @pl.when(i == last)
    def _():
        for blk in (i - 1, i):
            for j in range(EXPERT_BLOCK):
                scatter(blk, j, dst=0).wait()
        for j in range(EXPERT_BLOCK):
            gather(i + 1, j).wait()
        idx_copy(i + 2).wait()


def _moe_ffn(block_e, slot_idx, hn, w1g, w1l, b1g, b1l, w2, b2, n_rows):
    n_blocks = slot_idx.shape[0] - MOE_EXTRA_BLOCKS
    wspec = pl.BlockSpec((None, D_MODEL, D_FF), lambda i, be: (be[i], 0, 0))
    bspec = pl.BlockSpec((None, 1, D_FF), lambda i, be: (be[i], 0, 0))
    grid_spec = pltpu.PrefetchScalarGridSpec(
        num_scalar_prefetch=1,
        grid=(n_blocks,),
        in_specs=[
            pl.BlockSpec(memory_space=pl.ANY),
            pl.BlockSpec(memory_space=pl.ANY),
            wspec, wspec, bspec, bspec,
            pl.BlockSpec((None, D_FF, D_MODEL), lambda i, be: (be[i], 0, 0)),
            pl.BlockSpec((None, 1, D_MODEL), lambda i, be: (be[i], 0, 0)),
        ],
        out_specs=pl.BlockSpec(memory_space=pl.ANY),
        scratch_shapes=[
            pltpu.SMEM((IDX_RING, 2 * EXPERT_BLOCK), I32),
            pltpu.VMEM((ROW_BUFS, ROW_TILES, EXPERT_BLOCK, LANES), F32),
            pltpu.VMEM((ROW_BUFS, ROW_TILES, EXPERT_BLOCK, LANES), F32),
            pltpu.SemaphoreType.DMA((ROW_BUFS,)),
            pltpu.SemaphoreType.DMA((ROW_BUFS,)),
            pltpu.SemaphoreType.DMA((IDX_RING,)),
        ],
    )
    return pl.pallas_call(
        _moe_kernel,
        grid_spec=grid_spec,
        out_shape=jax.ShapeDtypeStruct(((n_rows + 2 * EXPERT_BLOCK) * ROW_TILES, LANES), F32),
        compiler_params=_cparams("arbitrary"),
        name="moe_ffn",
    )(block_e, slot_idx, hn, w1g, w1l, b1g, b1l, w2, b2)


def _combine_kernel(h_ref, y0_ref, y1_ref, y2_ref, y3_ref, meta_ref, fw_ref, o_ref):
    acc = h_ref[...]
    meta = meta_ref[...]
    for kk, y_ref in enumerate((y0_ref, y1_ref, y2_ref, y3_ref)):
        acc = acc + meta[:, TOP_K + kk:TOP_K + kk + 1] * _load_row_tiles(y_ref, acc.shape[0])
    o_ref[...] = _rms(acc, fw_ref[...])


def _combine(h, y_flat, meta, fw, tm):
    n = h.shape[0]
    tiles = n // tm
    row = lambda i: (i, 0)
    y_specs = [pl.BlockSpec((tm * ROW_TILES, LANES),
                            functools.partial(lambda i, kk: (kk * tiles + i, 0), kk=kk))
               for kk in range(TOP_K)]
    return pl.pallas_call(
        _combine_kernel,
        grid=(tiles,),
        in_specs=[pl.BlockSpec((tm, D_MODEL), row)] + y_specs + [
            pl.BlockSpec((tm, LANES), row),
            pl.BlockSpec((1, D_MODEL), lambda i: (0, 0)),
        ],
        out_specs=pl.BlockSpec((tm, D_MODEL), row),
        out_shape=jax.ShapeDtypeStruct((n, D_MODEL), F32),
        compiler_params=_cparams("parallel"),
        name="combine",
    )(h, y_flat, y_flat, y_flat, y_flat, meta, fw)


def _pad_heads_cols(w, heads):
    lead = w.shape[:-1]
    w = w.reshape(lead + (heads, HEAD_DIM))
    w = jnp.pad(w, [(0, 0)] * len(lead) + [(0, 0), (0, LANES - HEAD_DIM)])
    return w.reshape(lead + (heads * LANES,))


def _pick_tile(n, pref):
    t = pref
    while n % t:
        t //= 2
    return t


def kernel(x, mem, attn_norm_w, mem_norm_w, w_in, b_swa, rwkv_mu, rwkv_w0, rwkv_w2, rwkv_a0,
           rwkv_a2, rwkv_g2, rwkv_kk, rwkv_ka, rwkv_rk, rwkv_lnx_w, rwkv_lnx_b, swa_sinks,
           w_mem_kv, w_gate, w_branch, w_out, ffn_norm_w, w_router, b_router, w_exp1, b_exp1,
           w_exp2, b_exp2, final_norm_w):
    batch, seq, d = x.shape
    mem_len = mem.shape[1]
    n = batch * seq
    layer = 0
    x2 = x.reshape(n, d)
    row = lambda a: a.reshape(1, -1)

    w_in_l = w_in[layer]
    w_rw = w_in_l[:, :RWKV_PROJ].astype(BF16)
    sw = w_in_l[:, RWKV_PROJ:RWKV_PROJ + SWA_DIM + 2 * SWA_KV_DIM]
    bs = b_swa[layer]
    scale = HEAD_DIM ** -0.5
    w_sw = jnp.concatenate([
        _pad_heads_cols(sw[:, :SWA_DIM] * scale, SWA_Q_HEADS),
        _pad_heads_cols(sw[:, SWA_DIM:SWA_DIM + SWA_KV_DIM], SWA_KV_HEADS),
        _pad_heads_cols(sw[:, SWA_DIM + SWA_KV_DIM:], SWA_KV_HEADS)], axis=1).astype(BF16)
    b_sw = row(jnp.concatenate([
        _pad_heads_cols(bs[:SWA_DIM] * scale, SWA_Q_HEADS),
        _pad_heads_cols(bs[SWA_DIM:SWA_DIM + SWA_KV_DIM], SWA_KV_HEADS),
        _pad_heads_cols(bs[SWA_DIM + SWA_KV_DIM:], SWA_KV_HEADS)]))
    w_qm = w_in_l[:, RWKV_PROJ + SWA_DIM + 2 * SWA_KV_DIM:].astype(BF16)
    zeros_l = jnp.zeros((DECAY_LORA, RWKV_DIM), F32)
    wwa = jnp.concatenate([
        jnp.concatenate([rwkv_w2[layer], zeros_l], axis=1),
        jnp.concatenate([zeros_l, rwkv_a2[layer]], axis=1)], axis=0).astype(BF16)
    wb = w_branch[layer]
    wb1 = _pad_heads_cols(wb[1].T, SWA_Q_HEADS).T.astype(BF16)
    wr = jnp.pad(w_router[layer], ((0, 0), (0, LANES - N_EXPERTS)))
    wrh = wr.astype(BF16)
    wrl = (wr - wrh.astype(F32)).astype(BF16)
    br = row(jnp.pad(b_router[layer], (0, LANES - N_EXPERTS), constant_values=NEG_BIG))
    w1 = w_exp1[layer]
    b1 = b_exp1[layer]
    w1g, w1l = _w1_split(w1, LANES)
    b1g = b1[:, None, 0::2]
    b1l = b1[:, None, 1::2]
    w2 = w_exp2[layer].astype(BF16)
    b2 = b_exp2[layer][:, None, :]

    tm_proj = _pick_tile(seq, 512)
    p_rw, p_sw, q_mem = _inproj(x2, row(attn_norm_w[layer]), w_rw, w_sw, w_qm, b_sw, tm_proj)
    memkv = _memkv(mem.reshape(batch * mem_len, d), row(mem_norm_w[layer]),
                   w_mem_kv[layer].astype(BF16), mem_len)
    r, lw, k, v, kk, b, g, bonus = _rwkv_prep(
        p_rw, row(rwkv_mu[layer]), row(rwkv_w0[layer]), wwa, row(rwkv_a0[layer]),
        rwkv_g2[layer].astype(BF16), row(rwkv_kk[layer]), row(rwkv_ka[layer]),
        row(rwkv_rk[layer]), tm_proj, seq)
    y_rw = _rwkv_scan(r, lw, k, v, kk, b, g, bonus, row(rwkv_lnx_w[layer]),
                      row(rwkv_lnx_b[layer]), batch, seq, _pick_tile(seq, 256),
                      4 if batch % 4 == 0 else 1)
    y_sw = _swa(p_sw, swa_sinks[layer], batch, seq)

    tm_merge = _pick_tile(seq, 512)
    h, hn, meta, counts = _merge(
        x2, y_rw, y_sw, q_mem, memkv, row(attn_norm_w[layer]), w_gate[layer].astype(BF16),
        wb[0].astype(BF16), wb1, wb[2].astype(BF16), w_out[layer].astype(BF16),
        row(ffn_norm_w[layer]), wrh, wrl, br, tm_merge, seq, mem_len)

    a_total = n * TOP_K
    e_idx = meta[:, 0:TOP_K].astype(I32)
    pos = meta[:, 2 * TOP_K:3 * TOP_K].astype(I32)
    cnt = counts[0, :N_EXPERTS].astype(I32)
    padded = ((cnt + EXPERT_BLOCK - 1) // EXPERT_BLOCK) * EXPERT_BLOCK
    pad_ends = jnp.cumsum(padded)
    pad_starts = pad_ends - padded
    n_blocks = -(-a_total // EXPERT_BLOCK) + N_EXPERTS
    rows_idx = n_blocks + MOE_EXTRA_BLOCKS
    p_slots = rows_idx * EXPERT_BLOCK
    dest = (pad_starts[e_idx] + pos).reshape(-1)
    a_ids = jnp.arange(a_total, dtype=I32)
    slot_a = jnp.full((p_slots,), -1, I32).at[dest].set(a_ids)
    slot_id = jnp.arange(p_slots, dtype=I32)
    slot_tok = jnp.where(slot_a >= 0, slot_a // TOP_K, 0)
    dump = a_total + slot_id % (ROW_BUFS * EXPERT_BLOCK)
    slot_dst = jnp.where(slot_a >= 0, (slot_a % TOP_K) * n + slot_a // TOP_K, dump)
    slot_idx = jnp.concatenate([slot_tok.reshape(rows_idx, EXPERT_BLOCK),
                                slot_dst.reshape(rows_idx, EXPERT_BLOCK)], axis=1) * ROW_TILES
    block_start = jnp.arange(n_blocks, dtype=I32) * EXPERT_BLOCK
    block_e = jnp.minimum(jnp.sum(pad_ends[None, :] <= block_start[:, None], axis=1),
                          N_EXPERTS - 1).astype(I32)

    y_flat = _moe_ffn(block_e, slot_idx, hn, w1g, w1l, b1g, b1l, w2, b2, a_total)
    out = _combine(h, y_flat, meta, row(final_norm_w), tm_merge)
    return out.reshape(batch, seq, d)
```

```python
import functools

import jax
import jax.numpy as jnp
import numpy as np
from jax import lax
from jax.experimental import pallas as pl
from jax.experimental.pallas import tpu as pltpu

F32 = jnp.float32
BF16 = jnp.bfloat16
I32 = jnp.int32

D_MODEL = 1024
HEAD_DIM = 64
LANES = 128
RWKV_DIM = 512
RWKV_PAIRS = RWKV_DIM // LANES
DECAY_LORA = 64
AAA_LORA = 64
GATE_LORA = 128
RWKV_PROJ = 3 * RWKV_DIM + DECAY_LORA + AAA_LORA + GATE_LORA
LNX_EPS = 64e-5
SWA_Q_HEADS = 8
SWA_KV_HEADS = 2
SWA_GROUP = SWA_Q_HEADS // SWA_KV_HEADS
SWA_DIM = SWA_Q_HEADS * HEAD_DIM
SWA_KV_DIM = SWA_KV_HEADS * HEAD_DIM
SWA_QP = SWA_Q_HEADS * LANES
SWA_KVP = SWA_KV_HEADS * LANES
SWA_PROJ_P = SWA_QP + 2 * SWA_KVP
Q_BLOCK = 128
ALIBI_MAX = 8.0
MEM_HEADS = 4
MEM_HEAD_DIM = 128
MEM_DIM = MEM_HEADS * MEM_HEAD_DIM
N_EXPERTS = 32
TOP_K = 4
D_FF = 1024
SWIGLU_ALPHA = 1.702
SWIGLU_LIMIT = 7.0
EXPERT_BLOCK = 512
NORM_EPS = 1e-5
CHUNK = 64
NEG_BIG = -1e30

VMEM_LIMIT = 56 * 1024 * 1024


def _cparams(*sem):
    return pltpu.CompilerParams(dimension_semantics=sem, vmem_limit_bytes=VMEM_LIMIT)


def _rms(x, w):
    return x * lax.rsqrt(jnp.mean(x * x, axis=-1, keepdims=True) + NORM_EPS) * w


def _dot(a, b):
    return jnp.dot(a, b, preferred_element_type=F32)


def _dot_nt(a, b):
    return lax.dot_general(a, b, (((1,), (1,)), ((), ())), preferred_element_type=F32)


def _split2(x):
    hi = x.astype(BF16)
    lo = (x - hi.astype(F32)).astype(BF16)
    return hi, lo


def _split3(x):
    hi = x.astype(BF16)
    r1 = x - hi.astype(F32)
    mid = r1.astype(BF16)
    lo = (r1 - mid.astype(F32)).astype(BF16)
    return hi, mid, lo


U32 = jnp.uint32
ROW_WORDS = D_MODEL // 2
ROW_TILES = ROW_WORDS // LANES
HIGH_HALF = np.uint32(0xFFFF0000)


def _bf16_bits(x):
    return lax.bitcast_convert_type(x.astype(BF16).astype(F32), U32)


def _pack_rows(x):
    return (_bf16_bits(x[:, :ROW_WORDS]) >> 16) | (_bf16_bits(x[:, ROW_WORDS:]) & HIGH_HALF)


def _unpack_rows(w):
    lo = lax.bitcast_convert_type(w << 16, F32)
    hi = lax.bitcast_convert_type(w & HIGH_HALF, F32)
    return jnp.concatenate([lo, hi], axis=1)


def _store_row_tiles(ref, w):
    rows = w.shape[0]
    for c in range(ROW_TILES):
        ref[pl.ds(c, rows, stride=ROW_TILES), :] = w[:, c * LANES:(c + 1) * LANES]


def _load_row_tiles(ref, rows):
    return jnp.concatenate(
        [ref[pl.ds(c, rows, stride=ROW_TILES), :] for c in range(ROW_TILES)], axis=1)


def _group_ones(n, group):
    r = lax.broadcasted_iota(I32, (n, n), 0) // group
    c = lax.broadcasted_iota(I32, (n, n), 1) // group
    return (r == c).astype(BF16)


def _group_sum(x, ones):
    hi, lo = _split2(x)
    return _dot(hi, ones) + _dot(lo, ones)


def _inproj_kernel(x_ref, nw_ref, wr_ref, ws_ref, wq_ref, bs_ref, pr_ref, ps_ref, qm_ref):
    xb = _rms(x_ref[...], nw_ref[...]).astype(BF16)
    pr_ref[...] = _dot(xb, wr_ref[...])
    ps_ref[...] = (_dot(xb, ws_ref[...]) + bs_ref[...]).astype(BF16)
    qm_ref[...] = _dot(xb, wq_ref[...]).astype(BF16)


def _inproj(x2, nw, w_rw, w_sw, w_qm, b_sw, tm):
    n = x2.shape[0]
    const = lambda i: (0, 0)
    row = lambda i: (i, 0)
    return pl.pallas_call(
        _inproj_kernel,
        grid=(n // tm,),
        in_specs=[
            pl.BlockSpec((tm, D_MODEL), row),
            pl.BlockSpec((1, D_MODEL), const),
            pl.BlockSpec((D_MODEL, RWKV_PROJ), const),
            pl.BlockSpec((D_MODEL, SWA_PROJ_P), const),
            pl.BlockSpec((D_MODEL, MEM_DIM), const),
            pl.BlockSpec((1, SWA_PROJ_P), const),
        ],
        out_specs=[
            pl.BlockSpec((tm, RWKV_PROJ), row),
            pl.BlockSpec((tm, SWA_PROJ_P), row),
            pl.BlockSpec((tm, MEM_DIM), row),
        ],
        out_shape=[
            jax.ShapeDtypeStruct((n, RWKV_PROJ), F32),
            jax.ShapeDtypeStruct((n, SWA_PROJ_P), BF16),
            jax.ShapeDtypeStruct((n, MEM_DIM), BF16),
        ],
        compiler_params=_cparams("parallel"),
        name="inproj",
    )(x2, nw, w_rw, w_sw, w_qm, b_sw)


def _memkv_kernel(m_ref, nw_ref, w_ref, o_ref):
    mb = _rms(m_ref[...], nw_ref[...]).astype(BF16)
    o_ref[...] = _dot(mb, w_ref[...]).astype(BF16)


def _memkv(mem2, nw, w, tm):
    n = mem2.shape[0]
    return pl.pallas_call(
        _memkv_kernel,
        grid=(n // tm,),
        in_specs=[
            pl.BlockSpec((tm, D_MODEL), lambda i: (i, 0)),
            pl.BlockSpec((1, D_MODEL), lambda i: (0, 0)),
            pl.BlockSpec((D_MODEL, 2 * MEM_DIM), lambda i: (0, 0)),
        ],
        out_specs=pl.BlockSpec((tm, 2 * MEM_DIM), lambda i: (i, 0)),
        out_shape=jax.ShapeDtypeStruct((n, 2 * MEM_DIM), BF16),
        compiler_params=_cparams("parallel"),
        name="memkv",
    )(mem2, nw, w)


def _rwkv_prep_kernel(p_ref, halo_ref, mu_ref, w0_ref, wwa_ref, a0_ref, g2_ref, kk_ref, ka_ref,
                      rk_ref, r_out, lw_out, k_out, v_out, kk_out, b_out, g_out, bonus_out,
                      *, tiles_per_seq):
    i = pl.program_id(0)
    p = p_ref[...]
    tm = p.shape[0]
    halo = jnp.where(i % tiles_per_seq == 0, 0.0, halo_ref[7:8, :])
    row = lax.broadcasted_iota(I32, p.shape, 0)
    prev = jnp.where(row == 0, halo, pltpu.roll(p, 1, axis=0))
    ps = p + (prev - p) * mu_ref[...]
    r = ps[:, 0:RWKV_DIM]
    k = ps[:, RWKV_DIM:2 * RWKV_DIM]
    v = ps[:, 2 * RWKV_DIM:3 * RWKV_DIM]
    wa = ps[:, 3 * RWKV_DIM:3 * RWKV_DIM + LANES]
    pg = ps[:, 3 * RWKV_DIM + LANES:]
    lane = lax.broadcasted_iota(I32, (tm, LANES), 1)
    z = jnp.where(lane < DECAY_LORA, jnp.tanh(wa), wa).astype(BF16)
    lora = _dot(z, wwa_ref[...])
    w_log = -jax.nn.softplus(-(w0_ref[...] + lora[:, :RWKV_DIM])) - 0.5
    lw_out[...] = -jnp.exp(w_log)
    alr = jax.nn.sigmoid(a0_ref[...] + lora[:, RWKV_DIM:])
    g_out[...] = _dot(jax.nn.sigmoid(pg).astype(BF16), g2_ref[...])
    ones = _group_ones(RWKV_DIM, HEAD_DIM)
    kk = k * kk_ref[...]
    kk = kk / jnp.maximum(jnp.sqrt(_group_sum(kk * kk, ones)), 1e-12)
    k2 = k * (1.0 + (alr - 1.0) * ka_ref[...])
    r_out[...] = r
    k_out[...] = k2
    v_out[...] = v
    kk_out[...] = kk
    b_out[...] = kk * alr
    bonus_out[...] = _group_sum(r * k2 * rk_ref[...], ones) * v


def _rwkv_prep(p_rw, mu, w0, wwa, a0, g2, k_k, k_a, r_k, tm, seq):
    n = p_rw.shape[0]
    tiles_per_seq = seq // tm
    const = lambda i: (0, 0)
    row = lambda i: (i, 0)
    vec = pl.BlockSpec((1, RWKV_DIM), const)
    out = pl.BlockSpec((tm, RWKV_DIM), row)
    return pl.pallas_call(
        functools.partial(_rwkv_prep_kernel, tiles_per_seq=tiles_per_seq),
        grid=(n // tm,),
        in_specs=[
            pl.BlockSpec((tm, RWKV_PROJ), row),
            pl.BlockSpec((8, RWKV_PROJ), lambda i: (jnp.maximum(i * (tm // 8) - 1, 0), 0)),
            pl.BlockSpec((1, RWKV_PROJ), const),
            vec,
            pl.BlockSpec((LANES, 2 * RWKV_DIM), const),
            vec,
            pl.BlockSpec((GATE_LORA, RWKV_DIM), const),
            vec, vec, vec,
        ],
        out_specs=[out] * 8,
        out_shape=[jax.ShapeDtypeStruct((n, RWKV_DIM), F32)] * 8,
        compiler_params=_cparams("parallel"),
        name="rwkv_prep",
    )(p_rw, p_rw, mu, w0, wwa, a0, g2, k_k, k_a, r_k)


def _blockdiag(x, first_head):
    zero = jnp.zeros_like(x)
    return jnp.concatenate([jnp.where(first_head, x, zero), jnp.where(first_head, zero, x)], axis=0)


def _rwkv_scan_kernel(r_ref, lw_ref, k_ref, v_ref, kk_ref, b_ref, g_ref, bonus_ref, lnw_ref,
                      lnb_ref, y_ref, s_ref, *, chunks):
    @pl.when(pl.program_id(1) == 0)
    def _():
        s_ref[...] = jnp.zeros_like(s_ref)

    c2 = 2 * CHUNK
    tri_incl = (lax.broadcasted_iota(I32, (CHUNK, CHUNK), 1)
                <= lax.broadcasted_iota(I32, (CHUNK, CHUNK), 0)).astype(BF16)
    rowi = lax.broadcasted_iota(I32, (c2, c2), 0)
    coli = lax.broadcasted_iota(I32, (c2, c2), 1)
    strict = coli < rowi
    incl = coli <= rowi
    eye = (coli == rowi).astype(F32)
    first_head = lax.broadcasted_iota(I32, (CHUNK, LANES), 1) < HEAD_DIM
    ones = _group_ones(RWKV_DIM, HEAD_DIM)
    inv_n = 1.0 / HEAD_DIM

    seqs = r_ref.shape[0]
    pairs = range(RWKV_PAIRS)
    lanes = [slice(j * LANES, (j + 1) * LANES) for j in pairs]
    units = [(q, j) for q in range(seqs) for j in pairs]

    def chunk_body(c, carry):
        rows = pl.ds(pl.multiple_of(c * CHUNK, CHUNK), CHUNK)
        a_t, r_t, b_t, k_t, b_h, k_h, p_end, v = [], [], [], [], [], [], [], []
        for q in range(seqs):
            lw = lw_ref[q, rows, :]
            hi, mid, lo = _split3(lw)
            cl = _dot(tri_incl, hi) + _dot(tri_incl, mid) + _dot(tri_incl, lo)
            cl_end = cl[CHUNK - 1:CHUNK, :]
            kq = k_ref[q, rows, :]
            bq = b_ref[q, rows, :]
            inv = jnp.exp(-cl)
            to_end = jnp.exp(cl_end - cl)
            a_t.append(-kk_ref[q, rows, :] * jnp.exp(cl - lw))
            r_t.append(r_ref[q, rows, :] * jnp.exp(cl))
            b_t.append(bq * inv)
            k_t.append(kq * inv)
            b_h.append(bq * to_end)
            k_h.append(kq * to_end)
            p_end.append(jnp.exp(cl_end))
            v.append(v_ref[q, rows, :])
        bd = lambda x, u: _blockdiag(x[u[0]][:, lanes[u[1]]], first_head).astype(BF16)
        lhs = [jnp.concatenate([bd(a_t, u), bd(r_t, u)], axis=0) for u in units]
        a4 = [_dot_nt(lhs[i], jnp.concatenate([bd(b_t, u), bd(k_t, u)], axis=0))
              for i, u in enumerate(units)]
        ids = range(len(units))
        a_ab = [jnp.where(strict, a4[i][:c2, :c2], 0.0) for i in ids]
        tinv = [eye + a_ab[i] for i in ids]
        ab = [a_ab[i].astype(BF16) for i in ids]
        apow = [_dot(ab[i], ab[i]) for i in ids]
        for _ in range(4):
            st = [_dot(jnp.concatenate([tinv[i], apow[i]], axis=0).astype(BF16),
                       apow[i].astype(BF16)) for i in ids]
            tinv = [tinv[i] + st[i][:c2] for i in ids]
            apow = [st[i][c2:] for i in ids]
        tinv = [(tinv[i] + _dot(tinv[i].astype(BF16), apow[i].astype(BF16))).astype(BF16)
                for i in ids]
        v2 = [bd(v, u) for u in units]
        a_ak = [jnp.where(strict, a4[i][:c2, c2:], 0.0).astype(BF16) for i in ids]
        a_r = [jnp.concatenate([jnp.where(incl, a4[i][c2:, :c2], 0.0),
                                jnp.where(incl, a4[i][c2:, c2:], 0.0)], axis=1).astype(BF16)
               for i in ids]
        akv = [_dot(a_ak[i], v2[i]) for i in ids]
        s_old = [s_ref[i] for i in ids]
        su = [_dot_nt(lhs[i], s_old[i].astype(BF16)) for i in ids]
        sa = [_dot(tinv[i], (su[i][:c2] + akv[i]).astype(BF16)) for i in ids]
        y2 = [su[i][c2:] + _dot(a_r[i], jnp.concatenate([sa[i].astype(BF16), v2[i]], axis=0))
              for i in ids]
        sav_t = [jnp.concatenate([sa[i], v2[i].astype(F32)], axis=0).T.astype(BF16) for i in ids]
        for i, u in enumerate(units):
            s_ref[i] = s_old[i] * p_end[u[0]][:, lanes[u[1]]] + _dot(
                sav_t[i], jnp.concatenate([bd(b_h, u), bd(k_h, u)], axis=0))
        for q in range(seqs):
            y = jnp.concatenate([y2[q * RWKV_PAIRS + j][:CHUNK] + y2[q * RWKV_PAIRS + j][CHUNK:]
                                 for j in pairs], axis=1)
            m = _group_sum(y, ones) * inv_n
            d = y - m
            var = _group_sum(d * d, ones) * inv_n
            yn = d * lax.rsqrt(var + LNX_EPS) * lnw_ref[...] + lnb_ref[...]
            y_ref[q, rows, :] = ((yn + bonus_ref[q, rows, :]) * g_ref[q, rows, :]).astype(BF16)
        return carry

    lax.fori_loop(0, chunks, chunk_body, 0)


def _rwkv_scan(r, lw, k, v, kk, b, g, bonus, lnw, lnb, batch, seq, tc, seqs):
    as3 = lambda a: a.reshape(batch, seq, RWKV_DIM)
    blk = pl.BlockSpec((seqs, tc, RWKV_DIM), lambda bi, ti: (bi, ti, 0))
    vec = pl.BlockSpec((1, RWKV_DIM), lambda bi, ti: (0, 0))
    y = pl.pallas_call(
        functools.partial(_rwkv_scan_kernel, chunks=tc // CHUNK),
        grid=(batch // seqs, seq // tc),
        in_specs=[blk] * 8 + [vec, vec],
        out_specs=blk,
        out_shape=jax.ShapeDtypeStruct((batch, seq, RWKV_DIM), BF16),
        scratch_shapes=[pltpu.VMEM((seqs * RWKV_PAIRS, LANES, LANES), F32)],
        compiler_params=_cparams("parallel", "arbitrary"),
        name="rwkv_scan",
    )(*(as3(a) for a in (r, lw, k, v, kk, b, g, bonus)), lnw, lnb)
    return y.reshape(batch * seq, RWKV_DIM)


def _swa_kernel(sink_ref, q_ref, kvc_ref, kvp_ref, o_ref):
    nblk = pl.program_id(1)
    kv = jnp.concatenate([kvp_ref[...], kvc_ref[...]], axis=0)
    rows = SWA_GROUP * Q_BLOCK
    row = lax.broadcasted_iota(I32, (rows, 2 * Q_BLOCK), 0)
    gi_of_row = row // Q_BLOCK
    i = row % Q_BLOCK
    j = lax.broadcasted_iota(I32, (rows, 2 * Q_BLOCK), 1)
    dist = i + Q_BLOCK - j
    valid = (dist >= 0) & (dist < Q_BLOCK) & ((j >= Q_BLOCK) | (nblk > 0))
    distf = dist.astype(F32)
    gi_col = lax.broadcasted_iota(I32, (rows, 1), 0) // Q_BLOCK
    for h in range(SWA_KV_HEADS):
        kh = kv[:, h * LANES:(h + 1) * LANES]
        vh = kv[:, SWA_KVP + h * LANES:SWA_KVP + (h + 1) * LANES]
        heads = [h * SWA_GROUP + gi for gi in range(SWA_GROUP)]
        q4 = jnp.concatenate([q_ref[:, hq * LANES:(hq + 1) * LANES] for hq in heads], axis=0)
        bias = jnp.zeros((rows, 2 * Q_BLOCK), F32)
        sink = jnp.zeros((rows, 1), F32)
        for gi, hq in enumerate(heads):
            slope = 2.0 ** (-ALIBI_MAX * (hq + 1) / SWA_Q_HEADS)
            bias = jnp.where(gi_of_row == gi, slope * distf, bias)
            sink = jnp.where(gi_col == gi, sink_ref[hq], sink)
        s = jnp.where(valid, _dot_nt(q4, kh) - bias, NEG_BIG)
        m = jnp.maximum(jnp.max(s, axis=-1, keepdims=True), sink)
        e = jnp.exp(s - m)
        den = jnp.sum(e, axis=-1, keepdims=True) + jnp.exp(sink - m)
        o4 = _dot((e / den).astype(BF16), vh).astype(BF16)
        for gi, hq in enumerate(heads):
            o_ref[:, hq * LANES:(hq + 1) * LANES] = o4[gi * Q_BLOCK:(gi + 1) * Q_BLOCK]


def _swa(p_sw, sinks, batch, seq):
    n = p_sw.shape[0]
    nb = seq // Q_BLOCK
    kv_col = SWA_QP // (2 * SWA_KVP)
    return pl.pallas_call(
        _swa_kernel,
        grid=(batch, nb),
        in_specs=[
            pl.BlockSpec(memory_space=pltpu.SMEM),
            pl.BlockSpec((Q_BLOCK, SWA_QP), lambda bi, ni: (bi * nb + ni, 0)),
            pl.BlockSpec((Q_BLOCK, 2 * SWA_KVP), lambda bi, ni: (bi * nb + ni, kv_col)),
            pl.BlockSpec((Q_BLOCK, 2 * SWA_KVP),
                         lambda bi, ni: (bi * nb + jnp.maximum(ni - 1, 0), kv_col)),
        ],
        out_specs=pl.BlockSpec((Q_BLOCK, SWA_QP), lambda bi, ni: (bi * nb + ni, 0)),
        out_shape=jax.ShapeDtypeStruct((n, SWA_QP), BF16),
        compiler_params=_cparams("parallel", "parallel"),
        name="swa",
    )(sinks, p_sw, p_sw, p_sw)


def _merge_kernel(x_ref, yr_ref, ys_ref, qm_ref, km_ref, vm_ref, anw_ref, wg_ref, wb0_ref, wb1_ref,
                  wb2_ref, wo_ref, fnw_ref, wrh_ref, wrl_ref, br_ref,
                  h_ref, hn_ref, meta_ref, cnt_ref, carry_ref):
    @pl.when(pl.program_id(0) == 0)
    def _():
        carry_ref[...] = jnp.zeros_like(carry_ref)

    x = x_ref[...]
    tm = x.shape[0]
    xb = _rms(x, anw_ref[...]).astype(BF16)
    outs = []
    for hh in range(MEM_HEADS):
        cols = slice(hh * MEM_HEAD_DIM, (hh + 1) * MEM_HEAD_DIM)
        s = _dot_nt(qm_ref[:, cols], km_ref[:, cols]) * (MEM_HEAD_DIM ** -0.5)
        m = jnp.max(s, axis=-1, keepdims=True)
        e = jnp.exp(s - m)
        pr = (e / jnp.sum(e, axis=-1, keepdims=True)).astype(BF16)
        outs.append(_dot(pr, vm_ref[:, cols]))
    y_mem = jnp.concatenate(outs, axis=1).astype(BF16)
    merged = jax.nn.sigmoid(_dot(xb, wg_ref[0])) * _dot(yr_ref[...], wb0_ref[...])
    merged += jax.nn.sigmoid(_dot(xb, wg_ref[1])) * _dot(ys_ref[...], wb1_ref[...])
    merged += jax.nn.sigmoid(_dot(xb, wg_ref[2])) * _dot(y_mem, wb2_ref[...])
    h = x + _dot(merged.astype(BF16), wo_ref[...])
    h_ref[...] = h
    hn = _rms(h, fnw_ref[...])
    _store_row_tiles(hn_ref, _pack_rows(hn))
    hi, lo = _split2(hn)
    logits = _dot(hi, wrh_ref[...]) + _dot(hi, wrl_ref[...]) + _dot(lo, wrh_ref[...]) + br_ref[...]
    lane = lax.broadcasted_iota(I32, (tm, LANES), 1).astype(F32)
    work = logits
    vals, idxs, sels = [], [], []
    for _ in range(TOP_K):
        mk = jnp.max(work, axis=-1, keepdims=True)
        ik = jnp.min(jnp.where(work == mk, lane, float(LANES)), axis=-1, keepdims=True)
        sel = lane == ik
        work = jnp.where(sel, 2.0 * NEG_BIG, work)
        vals.append(mk)
        idxs.append(ik)
        sels.append(sel)
    es = [jnp.exp(vk - vals[0]) for vk in vals]
    den = es[0] + es[1] + es[2] + es[3]
    cnt = jnp.zeros((tm, LANES), F32)
    for sel in sels:
        cnt = cnt + sel.astype(F32)
    tri = (lax.broadcasted_iota(I32, (tm, tm), 1) < lax.broadcasted_iota(I32, (tm, tm), 0)).astype(BF16)
    prefix = _dot(tri, cnt.astype(BF16)) + carry_ref[0:1, :]
    meta = jnp.zeros((tm, LANES), F32)
    for kk in range(TOP_K):
        pos = jnp.sum(jnp.where(sels[kk], prefix, 0.0), axis=-1, keepdims=True)
        meta = jnp.where(lane == kk, idxs[kk], meta)
        meta = jnp.where(lane == TOP_K + kk, es[kk] / den, meta)
        meta = jnp.where(lane == 2 * TOP_K + kk, pos, meta)
    meta_ref[...] = meta
    total = carry_ref[0:1, :] + jnp.sum(cnt, axis=0, keepdims=True)
    carry_ref[...] = jnp.broadcast_to(total, carry_ref.shape)
    cnt_ref[...] = jnp.broadcast_to(total, cnt_ref.shape)


def _merge(x2, y_rw, y_sw, q_mem, memkv, anw, wg, wb0, wb1, wb2, wo, fnw, wrh, wrl, br, tm, seq,
           mem_len):
    n = x2.shape[0]
    tiles_per_seq = seq // tm
    row = lambda i: (i, 0)
    const = lambda i: (0, 0)
    return pl.pallas_call(
        _merge_kernel,
        grid=(n // tm,),
        in_specs=[
            pl.BlockSpec((tm, D_MODEL), row),
            pl.BlockSpec((tm, RWKV_DIM), row),
            pl.BlockSpec((tm, SWA_QP), row),
            pl.BlockSpec((tm, MEM_DIM), row),
            pl.BlockSpec((mem_len, MEM_DIM), lambda i: (i // tiles_per_seq, 0)),
            pl.BlockSpec((mem_len, MEM_DIM), lambda i: (i // tiles_per_seq, 1)),
            pl.BlockSpec((1, D_MODEL), const),
            pl.BlockSpec((3, D_MODEL, D_MODEL), lambda i: (0, 0, 0), pipeline_mode=pl.Buffered(1)),
            pl.BlockSpec((RWKV_DIM, D_MODEL), const, pipeline_mode=pl.Buffered(1)),
            pl.BlockSpec((SWA_QP, D_MODEL), const, pipeline_mode=pl.Buffered(1)),
            pl.BlockSpec((MEM_DIM, D_MODEL), const, pipeline_mode=pl.Buffered(1)),
            pl.BlockSpec((D_MODEL, D_MODEL), const, pipeline_mode=pl.Buffered(1)),
            pl.BlockSpec((1, D_MODEL), const),
            pl.BlockSpec((D_MODEL, LANES), const),
            pl.BlockSpec((D_MODEL, LANES), const),
            pl.BlockSpec((1, LANES), const),
        ],
        out_specs=[
            pl.BlockSpec((tm, D_MODEL), row),
            pl.BlockSpec((tm * ROW_TILES, LANES), row),
            pl.BlockSpec((tm, LANES), row),
            pl.BlockSpec((8, LANES), const),
        ],
        out_shape=[
            jax.ShapeDtypeStruct((n, D_MODEL), F32),
            jax.ShapeDtypeStruct((n * ROW_TILES, LANES), U32),
            jax.ShapeDtypeStruct((n, LANES), F32),
            jax.ShapeDtypeStruct((8, LANES), F32),
        ],
        scratch_shapes=[pltpu.VMEM((8, LANES), F32)],
        compiler_params=_cparams("arbitrary"),
        name="merge_router",
    )(x2, y_rw, y_sw, q_mem, memkv, memkv, anw, wg, wb0, wb1, wb2, wo, fnw, wrh, wrl, br)


def _w1_split_kernel(w_ref, g_ref, l_ref, t_ref):
    t_ref[...] = w_ref[...].T
    g_ref[...] = t_ref[pl.ds(0, D_FF, stride=2), :].T.astype(BF16)
    l_ref[...] = t_ref[pl.ds(1, D_FF, stride=2), :].T.astype(BF16)


def _w1_split(w1, rows):
    n_e = w1.shape[0]
    out = pl.BlockSpec((None, rows, D_FF), lambda e, r: (e, r, 0))
    return pl.pallas_call(
        _w1_split_kernel,
        grid=(n_e, D_MODEL // rows),
        in_specs=[pl.BlockSpec((None, rows, 2 * D_FF), lambda e, r: (e, r, 0))],
        out_specs=[out, out],
        out_shape=[jax.ShapeDtypeStruct((n_e, D_MODEL, D_FF), BF16)] * 2,
        scratch_shapes=[pltpu.VMEM((2 * D_FF, rows), F32)],
        compiler_params=_cparams("parallel", "parallel"),
        name="w1_split",
    )(w1)


IDX_RING = 4
ROW_BUFS = 2
MOE_EXTRA_BLOCKS = 2


def _moe_kernel(be_ref, idx_hbm, hn_hbm, w1g_ref, w1l_ref, b1g_ref, b1l_ref, w2_ref, b2_ref, y_hbm,
                idx_smem, xbuf, ybuf, gsem, ssem, isem):
    i = pl.program_id(0)
    last = pl.num_programs(0) - 1
    n_dump = y_hbm.shape[0] - 2 * EXPERT_BLOCK * ROW_TILES

    def tile_rows(start):
        return pl.ds(pl.multiple_of(start, ROW_TILES), ROW_TILES)

    def idx_copy(blk):
        ring = blk % IDX_RING
        return pltpu.make_async_copy(idx_hbm.at[blk], idx_smem.at[ring], isem.at[ring])

    def gather(blk, j):
        slot = blk % ROW_BUFS
        src = idx_smem[blk % IDX_RING, j]
        return pltpu.make_async_copy(hn_hbm.at[tile_rows(src), :],
                                     xbuf.at[slot, pl.ds(j * ROW_TILES, ROW_TILES), :], gsem.at[slot])

    def scatter(blk, j, dst=None):
        slot = blk % ROW_BUFS
        if dst is None:
            dst = idx_smem[blk % IDX_RING, EXPERT_BLOCK + j]
        return pltpu.make_async_copy(ybuf.at[slot, pl.ds(j * ROW_TILES, ROW_TILES), :],
                                     y_hbm.at[tile_rows(dst), :], ssem.at[slot])

    @pl.when(i == 0)
    def _():
        ybuf[...] = jnp.zeros_like(ybuf)
        for slot in range(ROW_BUFS):
            for j in range(EXPERT_BLOCK):
                scatter(slot, j, dst=n_dump + (slot * EXPERT_BLOCK + j) * ROW_TILES).start()
        idx_copy(0).start()
        idx_copy(0).wait()
        for j in range(EXPERT_BLOCK):
            gather(0, j).start()
        idx_copy(1).start()

    idx_copy(i + 1).wait()
    for j in range(EXPERT_BLOCK):
        gather(i + 1, j).start(priority=j % 2)
    for j in range(EXPERT_BLOCK):
        scatter(i, j, dst=0).wait()
    idx_copy(i + 2).start()
    for j in range(EXPERT_BLOCK):
        gather(i, j).wait()

    slot = i % ROW_BUFS
    xb = _unpack_rows(_load_row_tiles(xbuf.at[slot], EXPERT_BLOCK)).astype(BF16)
    glu = _dot(xb, w1g_ref[...]) + b1g_ref[...]
    lin = _dot(xb, w1l_ref[...]) + b1l_ref[...]
    glu = jnp.minimum(glu, SWIGLU_LIMIT)
    lin = jnp.clip(lin, -SWIGLU_LIMIT, SWIGLU_LIMIT)
    act = glu * jax.nn.sigmoid(SWIGLU_ALPHA * glu) * (lin + 1.0)
    yv = _dot(act.astype(BF16), w2_ref[...]) + b2_ref[...]
    _store_row_tiles(ybuf.at[slot], _pack_rows(yv))
    for j in range(EXPERT_BLOCK):
        scatter(i, j).start(priority=j % 2)

    @pl.when(i == last)
    def _():
        for blk in (i - 1, i):
            for j in range(EXPERT_BLOCK):
                scatter(blk, j, dst=0).wait()
        for j in range(EXPERT_BLOCK):
            gather(i + 1, j).wait()
        idx_copy(i + 2).wait()


def _moe_ffn(block_e, slot_idx, hn, w1g, w1l, b1g, b1l, w2, b2, n_rows):
    n_blocks = slot_idx.shape[0] - MOE_EXTRA_BLOCKS
    wspec = pl.BlockSpec((None, D_MODEL, D_FF), lambda i, be: (be[i], 0, 0))
    bspec = pl.BlockSpec((None, 1, D_FF), lambda i, be: (be[i], 0, 0))
    grid_spec = pltpu.PrefetchScalarGridSpec(
        num_scalar_prefetch=1,
        grid=(n_blocks,),
        in_specs=[
            pl.BlockSpec(memory_space=pl.ANY),
            pl.BlockSpec(memory_space=pl.ANY),
            wspec, wspec, bspec, bspec,
            pl.BlockSpec((None, D_FF, D_MODEL), lambda i, be: (be[i], 0, 0)),
            pl.BlockSpec((None, 1, D_MODEL), lambda i, be: (be[i], 0, 0)),
        ],
        out_specs=pl.BlockSpec(memory_space=pl.ANY),
        scratch_shapes=[
            pltpu.SMEM((IDX_RING, 2 * EXPERT_BLOCK), I32),
            pltpu.VMEM((ROW_BUFS, EXPERT_BLOCK * ROW_TILES, LANES), U32),
            pltpu.VMEM((ROW_BUFS, EXPERT_BLOCK * ROW_TILES, LANES), U32),
            pltpu.SemaphoreType.DMA((ROW_BUFS,)),
            pltpu.SemaphoreType.DMA((ROW_BUFS,)),
            pltpu.SemaphoreType.DMA((IDX_RING,)),
        ],
    )
    return pl.pallas_call(
        _moe_kernel,
        grid_spec=grid_spec,
        out_shape=jax.ShapeDtypeStruct(((n_rows + 2 * EXPERT_BLOCK) * ROW_TILES, LANES), U32),
        compiler_params=_cparams("arbitrary"),
        name="moe_ffn",
    )(block_e, slot_idx, hn, w1g, w1l, b1g, b1l, w2, b2)


def _combine_kernel(h_ref, y0_ref, y1_ref, y2_ref, y3_ref, meta_ref, fw_ref, o_ref):
    acc = h_ref[...]
    meta = meta_ref[...]
    for kk, y_ref in enumerate((y0_ref, y1_ref, y2_ref, y3_ref)):
        y = _unpack_rows(_load_row_tiles(y_ref, acc.shape[0]))
        acc = acc + meta[:, TOP_K + kk:TOP_K + kk + 1] * y
    o_ref[...] = _rms(acc, fw_ref[...])


def _combine(h, y_flat, meta, fw, tm):
    n = h.shape[0]
    tiles = n // tm
    row = lambda i: (i, 0)
    y_specs = [pl.BlockSpec((tm * ROW_TILES, LANES),
                            functools.partial(lambda i, kk: (kk * tiles + i, 0), kk=kk))
               for kk in range(TOP_K)]
    return pl.pallas_call(
        _combine_kernel,
        grid=(tiles,),
        in_specs=[pl.BlockSpec((tm, D_MODEL), row)] + y_specs + [
            pl.BlockSpec((tm, LANES), row),
            pl.BlockSpec((1, D_MODEL), lambda i: (0, 0)),
        ],
        out_specs=pl.BlockSpec((tm, D_MODEL), row),
        out_shape=jax.ShapeDtypeStruct((n, D_MODEL), F32),
        compiler_params=_cparams("parallel"),
        name="combine",
    )(h, y_flat, y_flat, y_flat, y_flat, meta, fw)


def _pad_heads_cols(w, heads):
    lead = w.shape[:-1]
    w = w.reshape(lead + (heads, HEAD_DIM))
    w = jnp.pad(w, [(0, 0)] * len(lead) + [(0, 0), (0, LANES - HEAD_DIM)])
    return w.reshape(lead + (heads * LANES,))


def _pick_tile(n, pref):
    t = pref
    while n % t:
        t //= 2
    return t


def kernel(x, mem, attn_norm_w, mem_norm_w, w_in, b_swa, rwkv_mu, rwkv_w0, rwkv_w2, rwkv_a0,
           rwkv_a2, rwkv_g2, rwkv_kk, rwkv_ka, rwkv_rk, rwkv_lnx_w, rwkv_lnx_b, swa_sinks,
           w_mem_kv, w_gate, w_branch, w_out, ffn_norm_w, w_router, b_router, w_exp1, b_exp1,
           w_exp2, b_exp2, final_norm_w):
    batch, seq, d = x.shape
    mem_len = mem.shape[1]
    n = batch * seq
    layer = 0
    x2 = x.reshape(n, d)
    row = lambda a: a.reshape(1, -1)

    w_in_l = w_in[layer]
    w_rw = w_in_l[:, :RWKV_PROJ].astype(BF16)
    sw = w_in_l[:, RWKV_PROJ:RWKV_PROJ + SWA_DIM + 2 * SWA_KV_DIM]
    bs = b_swa[layer]
    scale = HEAD_DIM ** -0.5
    w_sw = jnp.concatenate([
        _pad_heads_cols(sw[:, :SWA_DIM] * scale, SWA_Q_HEADS),
        _pad_heads_cols(sw[:, SWA_DIM:SWA_DIM + SWA_KV_DIM], SWA_KV_HEADS),
        _pad_heads_cols(sw[:, SWA_DIM + SWA_KV_DIM:], SWA_KV_HEADS)], axis=1).astype(BF16)
    b_sw = row(jnp.concatenate([
        _pad_heads_cols(bs[:SWA_DIM] * scale, SWA_Q_HEADS),
        _pad_heads_cols(bs[SWA_DIM:SWA_DIM + SWA_KV_DIM], SWA_KV_HEADS),
        _pad_heads_cols(bs[SWA_DIM + SWA_KV_DIM:], SWA_KV_HEADS)]))
    w_qm = w_in_l[:, RWKV_PROJ + SWA_DIM + 2 * SWA_KV_DIM:].astype(BF16)
    zeros_l = jnp.zeros((DECAY_LORA, RWKV_DIM), F32)
    wwa = jnp.concatenate([
        jnp.concatenate([rwkv_w2[layer], zeros_l], axis=1),
        jnp.concatenate([zeros_l, rwkv_a2[layer]], axis=1)], axis=0).astype(BF16)
    wb = w_branch[layer]
    wb1 = _pad_heads_cols(wb[1].T, SWA_Q_HEADS).T.astype(BF16)
    wr = jnp.pad(w_router[layer], ((0, 0), (0, LANES - N_EXPERTS)))
    wrh = wr.astype(BF16)
    wrl = (wr - wrh.astype(F32)).astype(BF16)
    br = row(jnp.pad(b_router[layer], (0, LANES - N_EXPERTS), constant_values=NEG_BIG))
    w1 = w_exp1[layer]
    b1 = b_exp1[layer]
    w1g, w1l = _w1_split(w1, LANES)
    b1g = b1[:, None, 0::2]
    b1l = b1[:, None, 1::2]
    w2 = w_exp2[layer].astype(BF16)
    b2 = b_exp2[layer][:, None, :]

    tm_proj = _pick_tile(seq, 512)
    p_rw, p_sw, q_mem = _inproj(x2, row(attn_norm_w[layer]), w_rw, w_sw, w_qm, b_sw, tm_proj)
    memkv = _memkv(mem.reshape(batch * mem_len, d), row(mem_norm_w[layer]),
                   w_mem_kv[layer].astype(BF16), mem_len)
    r, lw, k, v, kk, b, g, bonus = _rwkv_prep(
        p_rw, row(rwkv_mu[layer]), row(rwkv_w0[layer]), wwa, row(rwkv_a0[layer]),
        rwkv_g2[layer].astype(BF16), row(rwkv_kk[layer]), row(rwkv_ka[layer]),
        row(rwkv_rk[layer]), tm_proj, seq)
    y_rw = _rwkv_scan(r, lw, k, v, kk, b, g, bonus, row(rwkv_lnx_w[layer]),
                      row(rwkv_lnx_b[layer]), batch, seq, _pick_tile(seq, 256),
                      4 if batch % 4 == 0 else 1)
    y_sw = _swa(p_sw, swa_sinks[layer], batch, seq)

    tm_merge = _pick_tile(seq, 512)
    h, hn, meta, counts = _merge(
        x2, y_rw, y_sw, q_mem, memkv, row(attn_norm_w[layer]), w_gate[layer].astype(BF16),
        wb[0].astype(BF16), wb1, wb[2].astype(BF16), w_out[layer].astype(BF16),
        row(ffn_norm_w[layer]), wrh, wrl, br, tm_merge, seq, mem_len)

    a_total = n * TOP_K
    e_idx = meta[:, 0:TOP_K].astype(I32)
    pos = meta[:, 2 * TOP_K:3 * TOP_K].astype(I32)
    cnt = counts[0, :N_EXPERTS].astype(I32)
    padded = ((cnt + EXPERT_BLOCK - 1) // EXPERT_BLOCK) * EXPERT_BLOCK
    pad_ends = jnp.cumsum(padded)
    pad_starts = pad_ends - padded
    n_blocks = -(-a_total // EXPERT_BLOCK) + N_EXPERTS
    rows_idx = n_blocks + MOE_EXTRA_BLOCKS
    p_slots = rows_idx * EXPERT_BLOCK
    dest = (pad_starts[e_idx] + pos).reshape(-1)
    a_ids = jnp.arange(a_total, dtype=I32)
    slot_a = jnp.full((p_slots,), -1, I32).at[dest].set(a_ids)
    slot_id = jnp.arange(p_slots, dtype=I32)
    slot_tok = jnp.where(slot_a >= 0, slot_a // TOP_K, 0)
    dump = a_total + slot_id % (ROW_BUFS * EXPERT_BLOCK)
    slot_dst = jnp.where(slot_a >= 0, (slot_a % TOP_K) * n + slot_a // TOP_K, dump)
    slot_idx = jnp.concatenate([slot_tok.reshape(rows_idx, EXPERT_BLOCK),
                                slot_dst.reshape(rows_idx, EXPERT_BLOCK)], axis=1) * ROW_TILES
    block_start = jnp.arange(n_blocks, dtype=I32) * EXPERT_BLOCK
    block_e = jnp.minimum(jnp.sum(pad_ends[None, :] <= block_start[:, None], axis=1),
                          N_EXPERTS - 1).astype(I32)

    y_flat = _moe_ffn(block_e, slot_idx, hn, w1g, w1l, b1g, b1l, w2, b2, a_total)
    out = _combine(h, y_flat, meta, row(final_norm_w), tm_merge)
    return out.reshape(batch, seq, d)
```

```python
import functools

import jax
import jax.numpy as jnp
import numpy as np
from jax import lax
from jax.experimental import pallas as pl
from jax.experimental.pallas import tpu as pltpu

F32 = jnp.float32
BF16 = jnp.bfloat16
I32 = jnp.int32

D_MODEL = 1024
HEAD_DIM = 64
LANES = 128
RWKV_DIM = 512
RWKV_PAIRS = RWKV_DIM // LANES
DECAY_LORA = 64
AAA_LORA = 64
GATE_LORA = 128
RWKV_PROJ = 3 * RWKV_DIM + DECAY_LORA + AAA_LORA + GATE_LORA
LNX_EPS = 64e-5
SWA_Q_HEADS = 8
SWA_KV_HEADS = 2
SWA_GROUP = SWA_Q_HEADS // SWA_KV_HEADS
SWA_DIM = SWA_Q_HEADS * HEAD_DIM
SWA_KV_DIM = SWA_KV_HEADS * HEAD_DIM
SWA_QP = SWA_Q_HEADS * LANES
SWA_KVP = SWA_KV_HEADS * LANES
SWA_PROJ_P = SWA_QP + 2 * SWA_KVP
Q_BLOCK = 128
ALIBI_MAX = 8.0
MEM_HEADS = 4
MEM_HEAD_DIM = 128
MEM_DIM = MEM_HEADS * MEM_HEAD_DIM
N_EXPERTS = 32
TOP_K = 4
D_FF = 1024
SWIGLU_ALPHA = 1.702
SWIGLU_LIMIT = 7.0
EXPERT_BLOCK = 512
NORM_EPS = 1e-5
CHUNK = 64
NEG_BIG = -1e30

VMEM_LIMIT = 56 * 1024 * 1024


def _cparams(*sem):
    return pltpu.CompilerParams(dimension_semantics=sem, vmem_limit_bytes=VMEM_LIMIT)


def _rms(x, w):
    return x * lax.rsqrt(jnp.mean(x * x, axis=-1, keepdims=True) + NORM_EPS) * w


def _dot(a, b):
    return jnp.dot(a, b, preferred_element_type=F32)


def _dot_nt(a, b):
    return lax.dot_general(a, b, (((1,), (1,)), ((), ())), preferred_element_type=F32)


def _split2(x):
    hi = x.astype(BF16)
    lo = (x - hi.astype(F32)).astype(BF16)
    return hi, lo


def _split3(x):
    hi = x.astype(BF16)
    r1 = x - hi.astype(F32)
    mid = r1.astype(BF16)
    lo = (r1 - mid.astype(F32)).astype(BF16)
    return hi, mid, lo


U32 = jnp.uint32
ROW_WORDS = D_MODEL // 2
ROW_TILES = ROW_WORDS // LANES
HIGH_HALF = np.uint32(0xFFFF0000)


def _bf16_bits(x):
    return lax.bitcast_convert_type(x.astype(BF16).astype(F32), U32)


def _pack_rows(x):
    return (_bf16_bits(x[:, :ROW_WORDS]) >> 16) | (_bf16_bits(x[:, ROW_WORDS:]) & HIGH_HALF)


def _unpack_rows(w):
    lo = lax.bitcast_convert_type(w << 16, F32)
    hi = lax.bitcast_convert_type(w & HIGH_HALF, F32)
    return jnp.concatenate([lo, hi], axis=1)


def _store_row_tiles(ref, w):
    rows = w.shape[0]
    for c in range(ROW_TILES):
        ref[pl.ds(c, rows, stride=ROW_TILES), :] = w[:, c * LANES:(c + 1) * LANES]


def _load_row_tiles(ref, rows):
    return jnp.concatenate(
        [ref[pl.ds(c, rows, stride=ROW_TILES), :] for c in range(ROW_TILES)], axis=1)


def _group_ones(n, group):
    r = lax.broadcasted_iota(I32, (n, n), 0) // group
    c = lax.broadcasted_iota(I32, (n, n), 1) // group
    return (r == c).astype(BF16)


def _group_sum(x, ones):
    hi, lo = _split2(x)
    return _dot(hi, ones) + _dot(lo, ones)


def _inproj_kernel(x_ref, nw_ref, wr_ref, ws_ref, wq_ref, bs_ref, pr_ref, ps_ref, qm_ref):
    xb = _rms(x_ref[...], nw_ref[...]).astype(BF16)
    pr_ref[...] = _dot(xb, wr_ref[...])
    ps_ref[...] = (_dot(xb, ws_ref[...]) + bs_ref[...]).astype(BF16)
    qm_ref[...] = _dot(xb, wq_ref[...]).astype(BF16)


def _inproj(x2, nw, w_rw, w_sw, w_qm, b_sw, tm):
    n = x2.shape[0]
    const = lambda i: (0, 0)
    row = lambda i: (i, 0)
    return pl.pallas_call(
        _inproj_kernel,
        grid=(n // tm,),
        in_specs=[
            pl.BlockSpec((tm, D_MODEL), row),
            pl.BlockSpec((1, D_MODEL), const),
            pl.BlockSpec((D_MODEL, RWKV_PROJ), const),
            pl.BlockSpec((D_MODEL, SWA_PROJ_P), const),
            pl.BlockSpec((D_MODEL, MEM_DIM), const),
            pl.BlockSpec((1, SWA_PROJ_P), const),
        ],
        out_specs=[
            pl.BlockSpec((tm, RWKV_PROJ), row),
            pl.BlockSpec((tm, SWA_PROJ_P), row),
            pl.BlockSpec((tm, MEM_DIM), row),
        ],
        out_shape=[
            jax.ShapeDtypeStruct((n, RWKV_PROJ), F32),
            jax.ShapeDtypeStruct((n, SWA_PROJ_P), BF16),
            jax.ShapeDtypeStruct((n, MEM_DIM), BF16),
        ],
        compiler_params=_cparams("parallel"),
        name="inproj",
    )(x2, nw, w_rw, w_sw, w_qm, b_sw)


def _memkv_kernel(m_ref, nw_ref, w_ref, o_ref):
    mb = _rms(m_ref[...], nw_ref[...]).astype(BF16)
    o_ref[...] = _dot(mb, w_ref[...]).astype(BF16)


def _memkv(mem2, nw, w, tm):
    n = mem2.shape[0]
    return pl.pallas_call(
        _memkv_kernel,
        grid=(n // tm,),
        in_specs=[
            pl.BlockSpec((tm, D_MODEL), lambda i: (i, 0)),
            pl.BlockSpec((1, D_MODEL), lambda i: (0, 0)),
            pl.BlockSpec((D_MODEL, 2 * MEM_DIM), lambda i: (0, 0)),
        ],
        out_specs=pl.BlockSpec((tm, 2 * MEM_DIM), lambda i: (i, 0)),
        out_shape=jax.ShapeDtypeStruct((n, 2 * MEM_DIM), BF16),
        compiler_params=_cparams("parallel"),
        name="memkv",
    )(mem2, nw, w)


def _rwkv_prep_kernel(p_ref, halo_ref, mu_ref, w0_ref, wwa_ref, a0_ref, g2_ref, kk_ref, ka_ref,
                      rk_ref, r_out, lw_out, k_out, v_out, kk_out, b_out, g_out, bonus_out,
                      *, tiles_per_seq):
    i = pl.program_id(0)
    p = p_ref[...]
    tm = p.shape[0]
    halo = jnp.where(i % tiles_per_seq == 0, 0.0, halo_ref[7:8, :])
    row = lax.broadcasted_iota(I32, p.shape, 0)
    prev = jnp.where(row == 0, halo, pltpu.roll(p, 1, axis=0))
    ps = p + (prev - p) * mu_ref[...]
    r = ps[:, 0:RWKV_DIM]
    k = ps[:, RWKV_DIM:2 * RWKV_DIM]
    v = ps[:, 2 * RWKV_DIM:3 * RWKV_DIM]
    wa = ps[:, 3 * RWKV_DIM:3 * RWKV_DIM + LANES]
    pg = ps[:, 3 * RWKV_DIM + LANES:]
    lane = lax.broadcasted_iota(I32, (tm, LANES), 1)
    z = jnp.where(lane < DECAY_LORA, jnp.tanh(wa), wa).astype(BF16)
    lora = _dot(z, wwa_ref[...])
    w_log = -jax.nn.softplus(-(w0_ref[...] + lora[:, :RWKV_DIM])) - 0.5
    lw_out[...] = -jnp.exp(w_log)
    alr = jax.nn.sigmoid(a0_ref[...] + lora[:, RWKV_DIM:])
    g_out[...] = _dot(jax.nn.sigmoid(pg).astype(BF16), g2_ref[...])
    ones = _group_ones(RWKV_DIM, HEAD_DIM)
    kk = k * kk_ref[...]
    kk = kk / jnp.maximum(jnp.sqrt(_group_sum(kk * kk, ones)), 1e-12)
    k2 = k * (1.0 + (alr - 1.0) * ka_ref[...])
    r_out[...] = r
    k_out[...] = k2
    v_out[...] = v
    kk_out[...] = kk
    b_out[...] = kk * alr
    bonus_out[...] = _group_sum(r * k2 * rk_ref[...], ones) * v


def _rwkv_prep(p_rw, mu, w0, wwa, a0, g2, k_k, k_a, r_k, tm, seq):
    n = p_rw.shape[0]
    tiles_per_seq = seq // tm
    const = lambda i: (0, 0)
    row = lambda i: (i, 0)
    vec = pl.BlockSpec((1, RWKV_DIM), const)
    out = pl.BlockSpec((tm, RWKV_DIM), row)
    return pl.pallas_call(
        functools.partial(_rwkv_prep_kernel, tiles_per_seq=tiles_per_seq),
        grid=(n // tm,),
        in_specs=[
            pl.BlockSpec((tm, RWKV_PROJ), row),
            pl.BlockSpec((8, RWKV_PROJ), lambda i: (jnp.maximum(i * (tm // 8) - 1, 0), 0)),
            pl.BlockSpec((1, RWKV_PROJ), const),
            vec,
            pl.BlockSpec((LANES, 2 * RWKV_DIM), const),
            vec,
            pl.BlockSpec((GATE_LORA, RWKV_DIM), const),
            vec, vec, vec,
        ],
        out_specs=[out] * 8,
        out_shape=[jax.ShapeDtypeStruct((n, RWKV_DIM), F32)] * 8,
        compiler_params=_cparams("parallel"),
        name="rwkv_prep",
    )(p_rw, p_rw, mu, w0, wwa, a0, g2, k_k, k_a, r_k)


def _blockdiag(x, first_head):
    zero = jnp.zeros_like(x)
    return jnp.concatenate([jnp.where(first_head, x, zero), jnp.where(first_head, zero, x)], axis=0)


def _rwkv_scan_kernel(r_ref, lw_ref, k_ref, v_ref, kk_ref, b_ref, g_ref, bonus_ref, lnw_ref,
                      lnb_ref, y_ref, s_ref, *, chunks):
    @pl.when(pl.program_id(1) == 0)
    def _():
        s_ref[...] = jnp.zeros_like(s_ref)

    c2 = 2 * CHUNK
    tri_incl = (lax.broadcasted_iota(I32, (CHUNK, CHUNK), 1)
                <= lax.broadcasted_iota(I32, (CHUNK, CHUNK), 0)).astype(BF16)
    rowi = lax.broadcasted_iota(I32, (c2, c2), 0)
    coli = lax.broadcasted_iota(I32, (c2, c2), 1)
    strict = coli < rowi
    incl = coli <= rowi
    eye = (coli == rowi).astype(F32)
    first_head = lax.broadcasted_iota(I32, (CHUNK, LANES), 1) < HEAD_DIM
    ones = _group_ones(RWKV_DIM, HEAD_DIM)
    inv_n = 1.0 / HEAD_DIM

    seqs = r_ref.shape[0]
    pairs = range(RWKV_PAIRS)
    lanes = [slice(j * LANES, (j + 1) * LANES) for j in pairs]
    units = [(q, j) for q in range(seqs) for j in pairs]

    def chunk_body(c, carry):
        rows = pl.ds(pl.multiple_of(c * CHUNK, CHUNK), CHUNK)
        a_t, r_t, b_t, k_t, b_h, k_h, p_end, v = [], [], [], [], [], [], [], []
        for q in range(seqs):
            lw = lw_ref[q, rows, :]
            hi, mid, lo = _split3(lw)
            cl = _dot(tri_incl, hi) + _dot(tri_incl, mid) + _dot(tri_incl, lo)
            cl_end = cl[CHUNK - 1:CHUNK, :]
            kq = k_ref[q, rows, :]
            bq = b_ref[q, rows, :]
            inv = jnp.exp(-cl)
            to_end = jnp.exp(cl_end - cl)
            a_t.append(-kk_ref[q, rows, :] * jnp.exp(cl - lw))
            r_t.append(r_ref[q, rows, :] * jnp.exp(cl))
            b_t.append(bq * inv)
            k_t.append(kq * inv)
            b_h.append(bq * to_end)
            k_h.append(kq * to_end)
            p_end.append(jnp.exp(cl_end))
            v.append(v_ref[q, rows, :])
        bd = lambda x, u: _blockdiag(x[u[0]][:, lanes[u[1]]], first_head).astype(BF16)
        lhs = [jnp.concatenate([bd(a_t, u), bd(r_t, u)], axis=0) for u in units]
        a4 = [_dot_nt(lhs[i], jnp.concatenate([bd(b_t, u), bd(k_t, u)], axis=0))
              for i, u in enumerate(units)]
        ids = range(len(units))
        a_ab = [jnp.where(strict, a4[i][:c2, :c2], 0.0) for i in ids]
        tinv = [eye + a_ab[i] for i in ids]
        ab = [a_ab[i].astype(BF16) for i in ids]
        apow = [_dot(ab[i], ab[i]) for i in ids]
        for _ in range(4):
            st = [_dot(jnp.concatenate([tinv[i], apow[i]], axis=0).astype(BF16),
                       apow[i].astype(BF16)) for i in ids]
            tinv = [tinv[i] + st[i][:c2] for i in ids]
            apow = [st[i][c2:] for i in ids]
        tinv = [(tinv[i] + _dot(tinv[i].astype(BF16), apow[i].astype(BF16))).astype(BF16)
                for i in ids]
        v2 = [bd(v, u) for u in units]
        a_ak = [jnp.where(strict, a4[i][:c2, c2:], 0.0).astype(BF16) for i in ids]
        a_r = [jnp.concatenate([jnp.where(incl, a4[i][c2:, :c2], 0.0),
                                jnp.where(incl, a4[i][c2:, c2:], 0.0)], axis=1).astype(BF16)
               for i in ids]
        akv = [_dot(a_ak[i], v2[i]) for i in ids]
        s_old = [s_ref[i] for i in ids]
        su = [_dot_nt(lhs[i], s_old[i].astype(BF16)) for i in ids]
        sa = [_dot(tinv[i], (su[i][:c2] + akv[i]).astype(BF16)) for i in ids]
        y2 = [su[i][c2:] + _dot(a_r[i], jnp.concatenate([sa[i].astype(BF16), v2[i]], axis=0))
              for i in ids]
        sav_t = [jnp.concatenate([sa[i], v2[i].astype(F32)], axis=0).T.astype(BF16) for i in ids]
        for i, u in enumerate(units):
            s_ref[i] = s_old[i] * p_end[u[0]][:, lanes[u[1]]] + _dot(
                sav_t[i], jnp.concatenate([bd(b_h, u), bd(k_h, u)], axis=0))
        for q in range(seqs):
            y = jnp.concatenate([y2[q * RWKV_PAIRS + j][:CHUNK] + y2[q * RWKV_PAIRS + j][CHUNK:]
                                 for j in pairs], axis=1)
            m = _group_sum(y, ones) * inv_n
            d = y - m
            var = _group_sum(d * d, ones) * inv_n
            yn = d * lax.rsqrt(var + LNX_EPS) * lnw_ref[...] + lnb_ref[...]
            y_ref[q, rows, :] = ((yn + bonus_ref[q, rows, :]) * g_ref[q, rows, :]).astype(BF16)
        return carry

    lax.fori_loop(0, chunks, chunk_body, 0)


def _rwkv_scan(r, lw, k, v, kk, b, g, bonus, lnw, lnb, batch, seq, tc, seqs):
    as3 = lambda a: a.reshape(batch, seq, RWKV_DIM)
    blk = pl.BlockSpec((seqs, tc, RWKV_DIM), lambda bi, ti: (bi, ti, 0))
    vec = pl.BlockSpec((1, RWKV_DIM), lambda bi, ti: (0, 0))
    y = pl.pallas_call(
        functools.partial(_rwkv_scan_kernel, chunks=tc // CHUNK),
        grid=(batch // seqs, seq // tc),
        in_specs=[blk] * 8 + [vec, vec],
        out_specs=blk,
        out_shape=jax.ShapeDtypeStruct((batch, seq, RWKV_DIM), BF16),
        scratch_shapes=[pltpu.VMEM((seqs * RWKV_PAIRS, LANES, LANES), F32)],
        compiler_params=_cparams("parallel", "arbitrary"),
        name="rwkv_scan",
    )(*(as3(a) for a in (r, lw, k, v, kk, b, g, bonus)), lnw, lnb)
    return y.reshape(batch * seq, RWKV_DIM)


def _swa_kernel(sink_ref, q_ref, kvc_ref, kvp_ref, o_ref):
    nblk = pl.program_id(1)
    kv = jnp.concatenate([kvp_ref[...], kvc_ref[...]], axis=0)
    rows = SWA_GROUP * Q_BLOCK
    row = lax.broadcasted_iota(I32, (rows, 2 * Q_BLOCK), 0)
    gi_of_row = row // Q_BLOCK
    i = row % Q_BLOCK
    j = lax.broadcasted_iota(I32, (rows, 2 * Q_BLOCK), 1)
    dist = i + Q_BLOCK - j
    valid = (dist >= 0) & (dist < Q_BLOCK) & ((j >= Q_BLOCK) | (nblk > 0))
    distf = dist.astype(F32)
    gi_col = lax.broadcasted_iota(I32, (rows, 1), 0) // Q_BLOCK
    for h in range(SWA_KV_HEADS):
        kh = kv[:, h * LANES:(h + 1) * LANES]
        vh = kv[:, SWA_KVP + h * LANES:SWA_KVP + (h + 1) * LANES]
        heads = [h * SWA_GROUP + gi for gi in range(SWA_GROUP)]
        q4 = jnp.concatenate([q_ref[:, hq * LANES:(hq + 1) * LANES] for hq in heads], axis=0)
        bias = jnp.zeros((rows, 2 * Q_BLOCK), F32)
        sink = jnp.zeros((rows, 1), F32)
        for gi, hq in enumerate(heads):
            slope = 2.0 ** (-ALIBI_MAX * (hq + 1) / SWA_Q_HEADS)
            bias = jnp.where(gi_of_row == gi, slope * distf, bias)
            sink = jnp.where(gi_col == gi, sink_ref[hq], sink)
        s = jnp.where(valid, _dot_nt(q4, kh) - bias, NEG_BIG)
        m = jnp.maximum(jnp.max(s, axis=-1, keepdims=True), sink)
        e = jnp.exp(s - m)
        den = jnp.sum(e, axis=-1, keepdims=True) + jnp.exp(sink - m)
        o4 = _dot((e / den).astype(BF16), vh).astype(BF16)
        for gi, hq in enumerate(heads):
            o_ref[:, hq * LANES:(hq + 1) * LANES] = o4[gi * Q_BLOCK:(gi + 1) * Q_BLOCK]


def _swa(p_sw, sinks, batch, seq):
    n = p_sw.shape[0]
    nb = seq // Q_BLOCK
    kv_col = SWA_QP // (2 * SWA_KVP)
    return pl.pallas_call(
        _swa_kernel,
        grid=(batch, nb),
        in_specs=[
            pl.BlockSpec(memory_space=pltpu.SMEM),
            pl.BlockSpec((Q_BLOCK, SWA_QP), lambda bi, ni: (bi * nb + ni, 0)),
            pl.BlockSpec((Q_BLOCK, 2 * SWA_KVP), lambda bi, ni: (bi * nb + ni, kv_col)),
            pl.BlockSpec((Q_BLOCK, 2 * SWA_KVP),
                         lambda bi, ni: (bi * nb + jnp.maximum(ni - 1, 0), kv_col)),
        ],
        out_specs=pl.BlockSpec((Q_BLOCK, SWA_QP), lambda bi, ni: (bi * nb + ni, 0)),
        out_shape=jax.ShapeDtypeStruct((n, SWA_QP), BF16),
        compiler_params=_cparams("parallel", "parallel"),
        name="swa",
    )(sinks, p_sw, p_sw, p_sw)


def _merge_kernel(x_ref, yr_ref, ys_ref, qm_ref, km_ref, vm_ref, anw_ref, wg_ref, wb0_ref, wb1_ref,
                  wb2_ref, wo_ref, fnw_ref, wrh_ref, wrl_ref, br_ref,
                  h_ref, hn_ref, meta_ref, cnt_ref, carry_ref):
    @pl.when(pl.program_id(0) == 0)
    def _():
        carry_ref[...] = jnp.zeros_like(carry_ref)

    x = x_ref[...]
    tm = x.shape[0]
    xb = _rms(x, anw_ref[...]).astype(BF16)
    outs = []
    for hh in range(MEM_HEADS):
        cols = slice(hh * MEM_HEAD_DIM, (hh + 1) * MEM_HEAD_DIM)
        s = _dot_nt(qm_ref[:, cols], km_ref[:, cols]) * (MEM_HEAD_DIM ** -0.5)
        m = jnp.max(s, axis=-1, keepdims=True)
        e = jnp.exp(s - m)
        pr = (e / jnp.sum(e, axis=-1, keepdims=True)).astype(BF16)
        outs.append(_dot(pr, vm_ref[:, cols]))
    y_mem = jnp.concatenate(outs, axis=1).astype(BF16)
    merged = jax.nn.sigmoid(_dot(xb, wg_ref[0])) * _dot(yr_ref[...], wb0_ref[...])
    merged += jax.nn.sigmoid(_dot(xb, wg_ref[1])) * _dot(ys_ref[...], wb1_ref[...])
    merged += jax.nn.sigmoid(_dot(xb, wg_ref[2])) * _dot(y_mem, wb2_ref[...])
    h = x + _dot(merged.astype(BF16), wo_ref[...])
    h_ref[...] = h
    hn = _rms(h, fnw_ref[...])
    _store_row_tiles(hn_ref, _pack_rows(hn))
    hi, lo = _split2(hn)
    logits = _dot(hi, wrh_ref[...]) + _dot(hi, wrl_ref[...]) + _dot(lo, wrh_ref[...]) + br_ref[...]
    lane = lax.broadcasted_iota(I32, (tm, LANES), 1).astype(F32)
    work = logits
    vals, idxs, sels = [], [], []
    for _ in range(TOP_K):
        mk = jnp.max(work, axis=-1, keepdims=True)
        ik = jnp.min(jnp.where(work == mk, lane, float(LANES)), axis=-1, keepdims=True)
        sel = lane == ik
        work = jnp.where(sel, 2.0 * NEG_BIG, work)
        vals.append(mk)
        idxs.append(ik)
        sels.append(sel)
    es = [jnp.exp(vk - vals[0]) for vk in vals]
    den = es[0] + es[1] + es[2] + es[3]
    cnt = jnp.zeros((tm, LANES), F32)
    for sel in sels:
        cnt = cnt + sel.astype(F32)
    tri = (lax.broadcasted_iota(I32, (tm, tm), 1) < lax.broadcasted_iota(I32, (tm, tm), 0)).astype(BF16)
    prefix = _dot(tri, cnt.astype(BF16)) + carry_ref[0:1, :]
    meta = jnp.zeros((tm, LANES), F32)
    for kk in range(TOP_K):
        pos = jnp.sum(jnp.where(sels[kk], prefix, 0.0), axis=-1, keepdims=True)
        meta = jnp.where(lane == kk, idxs[kk], meta)
        meta = jnp.where(lane == TOP_K + kk, es[kk] / den, meta)
        meta = jnp.where(lane == 2 * TOP_K + kk, pos, meta)
    meta_ref[...] = meta
    total = carry_ref[0:1, :] + jnp.sum(cnt, axis=0, keepdims=True)
    carry_ref[...] = jnp.broadcast_to(total, carry_ref.shape)
    cnt_ref[...] = jnp.broadcast_to(total, cnt_ref.shape)


def _merge(x2, y_rw, y_sw, q_mem, memkv, anw, wg, wb0, wb1, wb2, wo, fnw, wrh, wrl, br, tm, seq,
           mem_len):
    n = x2.shape[0]
    tiles_per_seq = seq // tm
    row = lambda i: (i, 0)
    const = lambda i: (0, 0)
    return pl.pallas_call(
        _merge_kernel,
        grid=(n // tm,),
        in_specs=[
            pl.BlockSpec((tm, D_MODEL), row),
            pl.BlockSpec((tm, RWKV_DIM), row),
            pl.BlockSpec((tm, SWA_QP), row),
            pl.BlockSpec((tm, MEM_DIM), row),
            pl.BlockSpec((mem_len, MEM_DIM), lambda i: (i // tiles_per_seq, 0)),
            pl.BlockSpec((mem_len, MEM_DIM), lambda i: (i // tiles_per_seq, 1)),
            pl.BlockSpec((1, D_MODEL), const),
            pl.BlockSpec((3, D_MODEL, D_MODEL), lambda i: (0, 0, 0), pipeline_mode=pl.Buffered(1)),
            pl.BlockSpec((RWKV_DIM, D_MODEL), const, pipeline_mode=pl.Buffered(1)),
            pl.BlockSpec((SWA_QP, D_MODEL), const, pipeline_mode=pl.Buffered(1)),
            pl.BlockSpec((MEM_DIM, D_MODEL), const, pipeline_mode=pl.Buffered(1)),
            pl.BlockSpec((D_MODEL, D_MODEL), const, pipeline_mode=pl.Buffered(1)),
            pl.BlockSpec((1, D_MODEL), const),
            pl.BlockSpec((D_MODEL, LANES), const),
            pl.BlockSpec((D_MODEL, LANES), const),
            pl.BlockSpec((1, LANES), const),
        ],
        out_specs=[
            pl.BlockSpec((tm, D_MODEL), row),
            pl.BlockSpec((tm * ROW_TILES, LANES), row),
            pl.BlockSpec((tm, LANES), row),
            pl.BlockSpec((8, LANES), const),
        ],
        out_shape=[
            jax.ShapeDtypeStruct((n, D_MODEL), F32),
            jax.ShapeDtypeStruct((n * ROW_TILES, LANES), U32),
            jax.ShapeDtypeStruct((n, LANES), F32),
            jax.ShapeDtypeStruct((8, LANES), F32),
        ],
        scratch_shapes=[pltpu.VMEM((8, LANES), F32)],
        compiler_params=_cparams("arbitrary"),
        name="merge_router",
    )(x2, y_rw, y_sw, q_mem, memkv, memkv, anw, wg, wb0, wb1, wb2, wo, fnw, wrh, wrl, br)


def _stage_expert_weights(w1_ref, w2_ref, w1g_s, w1l_s, w2_s, t_s):
    for r in range(D_MODEL // LANES):
        rows = slice(r * LANES, (r + 1) * LANES)
        t_s[...] = w1_ref[rows, :].T
        w1g_s[rows, :] = t_s[pl.ds(0, D_FF, stride=2), :].T.astype(BF16)
        w1l_s[rows, :] = t_s[pl.ds(1, D_FF, stride=2), :].T.astype(BF16)
    w2_s[...] = w2_ref[...].astype(BF16)


IDX_RING = 4
ROW_BUFS = 2
MOE_EXTRA_BLOCKS = 2


def _moe_kernel(be_ref, idx_hbm, hn_hbm, w1_ref, b1g_ref, b1l_ref, w2_ref, b2_ref, y_hbm,
                idx_smem, xbuf, ybuf, w1g_s, w1l_s, w2_s, t_s, gsem, ssem, isem):
    i = pl.program_id(0)
    last = pl.num_programs(0) - 1
    n_dump = y_hbm.shape[0] - 2 * EXPERT_BLOCK * ROW_TILES

    def tile_rows(start):
        return pl.ds(pl.multiple_of(start, ROW_TILES), ROW_TILES)

    def idx_copy(blk):
        ring = blk % IDX_RING
        return pltpu.make_async_copy(idx_hbm.at[blk], idx_smem.at[ring], isem.at[ring])

    def gather(blk, j):
        slot = blk % ROW_BUFS
        src = idx_smem[blk % IDX_RING, j]
        return pltpu.make_async_copy(hn_hbm.at[tile_rows(src), :],
                                     xbuf.at[slot, pl.ds(j * ROW_TILES, ROW_TILES), :], gsem.at[slot])

    def scatter(blk, j, dst=None):
        slot = blk % ROW_BUFS
        if dst is None:
            dst = idx_smem[blk % IDX_RING, EXPERT_BLOCK + j]
        return pltpu.make_async_copy(ybuf.at[slot, pl.ds(j * ROW_TILES, ROW_TILES), :],
                                     y_hbm.at[tile_rows(dst), :], ssem.at[slot])

    @pl.when(i == 0)
    def _():
        ybuf[...] = jnp.zeros_like(ybuf)
        for slot in range(ROW_BUFS):
            for j in range(EXPERT_BLOCK):
                scatter(slot, j, dst=n_dump + (slot * EXPERT_BLOCK + j) * ROW_TILES).start()
        idx_copy(0).start()
        idx_copy(0).wait()
        for j in range(EXPERT_BLOCK):
            gather(0, j).start()
        idx_copy(1).start()

    idx_copy(i + 1).wait()
    for j in range(EXPERT_BLOCK):
        gather(i + 1, j).start(priority=j % 2)
    for j in range(EXPERT_BLOCK):
        scatter(i, j, dst=0).wait()
    idx_copy(i + 2).start()

    @pl.when((i == 0) | (be_ref[i] != be_ref[jnp.maximum(i - 1, 0)]))
    def _():
        _stage_expert_weights(w1_ref, w2_ref, w1g_s, w1l_s, w2_s, t_s)

    for j in range(EXPERT_BLOCK):
        gather(i, j).wait()

    slot = i % ROW_BUFS
    xb = _unpack_rows(_load_row_tiles(xbuf.at[slot], EXPERT_BLOCK)).astype(BF16)
    glu = _dot(xb, w1g_s[...]) + b1g_ref[...]
    lin = _dot(xb, w1l_s[...]) + b1l_ref[...]
    glu = jnp.minimum(glu, SWIGLU_LIMIT)
    lin = jnp.clip(lin, -SWIGLU_LIMIT, SWIGLU_LIMIT)
    act = glu * jax.nn.sigmoid(SWIGLU_ALPHA * glu) * (lin + 1.0)
    yv = _dot(act.astype(BF16), w2_s[...]) + b2_ref[...]
    _store_row_tiles(ybuf.at[slot], _pack_rows(yv))
    for j in range(EXPERT_BLOCK):
        scatter(i, j).start(priority=j % 2)

    @pl.when(i == last)
    def _():
        for blk in (i - 1, i):
            for j in range(EXPERT_BLOCK):
                scatter(blk, j, dst=0).wait()
        for j in range(EXPERT_BLOCK):
            gather(i + 1, j).wait()
        idx_copy(i + 2).wait()


def _moe_ffn(block_e, slot_idx, hn, w1, b1g, b1l, w2, b2, n_rows):
    n_blocks = slot_idx.shape[0] - MOE_EXTRA_BLOCKS
    bspec = pl.BlockSpec((None, 1, D_FF), lambda i, be: (be[i], 0, 0))
    grid_spec = pltpu.PrefetchScalarGridSpec(
        num_scalar_prefetch=1,
        grid=(n_blocks,),
        in_specs=[
            pl.BlockSpec(memory_space=pl.ANY),
            pl.BlockSpec(memory_space=pl.ANY),
            pl.BlockSpec((None, D_MODEL, 2 * D_FF), lambda i, be: (be[i], 0, 0)),
            bspec, bspec,
            pl.BlockSpec((None, D_FF, D_MODEL), lambda i, be: (be[i], 0, 0)),
            pl.BlockSpec((None, 1, D_MODEL), lambda i, be: (be[i], 0, 0)),
        ],
        out_specs=pl.BlockSpec(memory_space=pl.ANY),
        scratch_shapes=[
            pltpu.SMEM((IDX_RING, 2 * EXPERT_BLOCK), I32),
            pltpu.VMEM((ROW_BUFS, EXPERT_BLOCK * ROW_TILES, LANES), U32),
            pltpu.VMEM((ROW_BUFS, EXPERT_BLOCK * ROW_TILES, LANES), U32),
            pltpu.VMEM((D_MODEL, D_FF), BF16),
            pltpu.VMEM((D_MODEL, D_FF), BF16),
            pltpu.VMEM((D_FF, D_MODEL), BF16),
            pltpu.VMEM((2 * D_FF, LANES), F32),
            pltpu.SemaphoreType.DMA((ROW_BUFS,)),
            pltpu.SemaphoreType.DMA((ROW_BUFS,)),
            pltpu.SemaphoreType.DMA((IDX_RING,)),
        ],
    )
    return pl.pallas_call(
        _moe_kernel,
        grid_spec=grid_spec,
        out_shape=jax.ShapeDtypeStruct(((n_rows + 2 * EXPERT_BLOCK) * ROW_TILES, LANES), U32),
        compiler_params=_cparams("arbitrary"),
        name="moe_ffn",
    )(block_e, slot_idx, hn, w1, b1g, b1l, w2, b2)


def _combine_kernel(h_ref, y0_ref, y1_ref, y2_ref, y3_ref, meta_ref, fw_ref, o_ref):
    acc = h_ref[...]
    meta = meta_ref[...]
    for kk, y_ref in enumerate((y0_ref, y1_ref, y2_ref, y3_ref)):
        y = _unpack_rows(_load_row_tiles(y_ref, acc.shape[0]))
        acc = acc + meta[:, TOP_K + kk:TOP_K + kk + 1] * y
    o_ref[...] = _rms(acc, fw_ref[...])


def _combine(h, y_flat, meta, fw, tm):
    n = h.shape[0]
    tiles = n // tm
    row = lambda i: (i, 0)
    y_specs = [pl.BlockSpec((tm * ROW_TILES, LANES),
                            functools.partial(lambda i, kk: (kk * tiles + i, 0), kk=kk))
               for kk in range(TOP_K)]
    return pl.pallas_call(
        _combine_kernel,
        grid=(tiles,),
        in_specs=[pl.BlockSpec((tm, D_MODEL), row)] + y_specs + [
            pl.BlockSpec((tm, LANES), row),
            pl.BlockSpec((1, D_MODEL), lambda i: (0, 0)),
        ],
        out_specs=pl.BlockSpec((tm, D_MODEL), row),
        out_shape=jax.ShapeDtypeStruct((n, D_MODEL), F32),
        compiler_params=_cparams("parallel"),
        name="combine",
    )(h, y_flat, y_flat, y_flat, y_flat, meta, fw)


def _pad_heads_cols(w, heads):
    lead = w.shape[:-1]
    w = w.reshape(lead + (heads, HEAD_DIM))
    w = jnp.pad(w, [(0, 0)] * len(lead) + [(0, 0), (0, LANES - HEAD_DIM)])
    return w.reshape(lead + (heads * LANES,))


def _pick_tile(n, pref):
    t = pref
    while n % t:
        t //= 2
    return t


def kernel(x, mem, attn_norm_w, mem_norm_w, w_in, b_swa, rwkv_mu, rwkv_w0, rwkv_w2, rwkv_a0,
           rwkv_a2, rwkv_g2, rwkv_kk, rwkv_ka, rwkv_rk, rwkv_lnx_w, rwkv_lnx_b, swa_sinks,
           w_mem_kv, w_gate, w_branch, w_out, ffn_norm_w, w_router, b_router, w_exp1, b_exp1,
           w_exp2, b_exp2, final_norm_w):
    batch, seq, d = x.shape
    mem_len = mem.shape[1]
    n = batch * seq
    layer = 0
    x2 = x.reshape(n, d)
    row = lambda a: a.reshape(1, -1)

    w_in_l = w_in[layer]
    w_rw = w_in_l[:, :RWKV_PROJ].astype(BF16)
    sw = w_in_l[:, RWKV_PROJ:RWKV_PROJ + SWA_DIM + 2 * SWA_KV_DIM]
    bs = b_swa[layer]
    scale = HEAD_DIM ** -0.5
    w_sw = jnp.concatenate([
        _pad_heads_cols(sw[:, :SWA_DIM] * scale, SWA_Q_HEADS),
        _pad_heads_cols(sw[:, SWA_DIM:SWA_DIM + SWA_KV_DIM], SWA_KV_HEADS),
        _pad_heads_cols(sw[:, SWA_DIM + SWA_KV_DIM:], SWA_KV_HEADS)], axis=1).astype(BF16)
    b_sw = row(jnp.concatenate([
        _pad_heads_cols(bs[:SWA_DIM] * scale, SWA_Q_HEADS),
        _pad_heads_cols(bs[SWA_DIM:SWA_DIM + SWA_KV_DIM], SWA_KV_HEADS),
        _pad_heads_cols(bs[SWA_DIM + SWA_KV_DIM:], SWA_KV_HEADS)]))
    w_qm = w_in_l[:, RWKV_PROJ + SWA_DIM + 2 * SWA_KV_DIM:].astype(BF16)
    zeros_l = jnp.zeros((DECAY_LORA, RWKV_DIM), F32)
    wwa = jnp.concatenate([
        jnp.concatenate([rwkv_w2[layer], zeros_l], axis=1),
        jnp.concatenate([zeros_l, rwkv_a2[layer]], axis=1)], axis=0).astype(BF16)
    wb = w_branch[layer]
    wb1 = _pad_heads_cols(wb[1].T, SWA_Q_HEADS).T.astype(BF16)
    wr = jnp.pad(w_router[layer], ((0, 0), (0, LANES - N_EXPERTS)))
    wrh = wr.astype(BF16)
    wrl = (wr - wrh.astype(F32)).astype(BF16)
    br = row(jnp.pad(b_router[layer], (0, LANES - N_EXPERTS), constant_values=NEG_BIG))
    w1 = w_exp1[layer]
    b1 = b_exp1[layer]
    b1g = b1[:, None, 0::2]
    b1l = b1[:, None, 1::2]
    w2 = w_exp2[layer]
    b2 = b_exp2[layer][:, None, :]

    tm_proj = _pick_tile(seq, 512)
    p_rw, p_sw, q_mem = _inproj(x2, row(attn_norm_w[layer]), w_rw, w_sw, w_qm, b_sw, tm_proj)
    memkv = _memkv(mem.reshape(batch * mem_len, d), row(mem_norm_w[layer]),
                   w_mem_kv[layer].astype(BF16), mem_len)
    r, lw, k, v, kk, b, g, bonus = _rwkv_prep(
        p_rw, row(rwkv_mu[layer]), row(rwkv_w0[layer]), wwa, row(rwkv_a0[layer]),
        rwkv_g2[layer].astype(BF16), row(rwkv_kk[layer]), row(rwkv_ka[layer]),
        row(rwkv_rk[layer]), tm_proj, seq)
    y_rw = _rwkv_scan(r, lw, k, v, kk, b, g, bonus, row(rwkv_lnx_w[layer]),
                      row(rwkv_lnx_b[layer]), batch, seq, _pick_tile(seq, 256),
                      4 if batch % 4 == 0 else 1)
    y_sw = _swa(p_sw, swa_sinks[layer], batch, seq)

    tm_merge = _pick_tile(seq, 512)
    h, hn, meta, counts = _merge(
        x2, y_rw, y_sw, q_mem, memkv, row(attn_norm_w[layer]), w_gate[layer].astype(BF16),
        wb[0].astype(BF16), wb1, wb[2].astype(BF16), w_out[layer].astype(BF16),
        row(ffn_norm_w[layer]), wrh, wrl, br, tm_merge, seq, mem_len)

    a_total = n * TOP_K
    e_idx = meta[:, 0:TOP_K].astype(I32)
    pos = meta[:, 2 * TOP_K:3 * TOP_K].astype(I32)
    cnt = counts[0, :N_EXPERTS].astype(I32)
    padded = ((cnt + EXPERT_BLOCK - 1) // EXPERT_BLOCK) * EXPERT_BLOCK
    pad_ends = jnp.cumsum(padded)
    pad_starts = pad_ends - padded
    n_blocks = -(-a_total // EXPERT_BLOCK) + N_EXPERTS
    rows_idx = n_blocks + MOE_EXTRA_BLOCKS
    p_slots = rows_idx * EXPERT_BLOCK
    dest = (pad_starts[e_idx] + pos).reshape(-1)
    a_ids = jnp.arange(a_total, dtype=I32)
    slot_a = jnp.full((p_slots,), -1, I32).at[dest].set(a_ids)
    slot_id = jnp.arange(p_slots, dtype=I32)
    slot_tok = jnp.where(slot_a >= 0, slot_a // TOP_K, 0)
    dump = a_total + slot_id % (ROW_BUFS * EXPERT_BLOCK)
    slot_dst = jnp.where(slot_a >= 0, (slot_a % TOP_K) * n + slot_a // TOP_K, dump)
    slot_idx = jnp.concatenate([slot_tok.reshape(rows_idx, EXPERT_BLOCK),
                                slot_dst.reshape(rows_idx, EXPERT_BLOCK)], axis=1) * ROW_TILES
    block_start = jnp.arange(n_blocks, dtype=I32) * EXPERT_BLOCK
    block_e = jnp.minimum(jnp.sum(pad_ends[None, :] <= block_start[:, None], axis=1),
                          N_EXPERTS - 1).astype(I32)

    y_flat = _moe_ffn(block_e, slot_idx, hn, w1, b1g, b1l, w2, b2, a_total)
    out = _combine(h, y_flat, meta, row(final_norm_w), tm_merge)
    return out.reshape(batch, seq, d)
```

```python
import functools

import jax
import jax.numpy as jnp
import numpy as np
from jax import lax
from jax.experimental import pallas as pl
from jax.experimental.pallas import tpu as pltpu

F32 = jnp.float32
BF16 = jnp.bfloat16
I32 = jnp.int32

D_MODEL = 1024
HEAD_DIM = 64
LANES = 128
RWKV_DIM = 512
RWKV_PAIRS = RWKV_DIM // LANES
DECAY_LORA = 64
AAA_LORA = 64
GATE_LORA = 128
RWKV_PROJ = 3 * RWKV_DIM + DECAY_LORA + AAA_LORA + GATE_LORA
LNX_EPS = 64e-5
SWA_Q_HEADS = 8
SWA_KV_HEADS = 2
SWA_GROUP = SWA_Q_HEADS // SWA_KV_HEADS
SWA_DIM = SWA_Q_HEADS * HEAD_DIM
SWA_KV_DIM = SWA_KV_HEADS * HEAD_DIM
SWA_QP = SWA_Q_HEADS * LANES
SWA_KVP = SWA_KV_HEADS * LANES
SWA_PROJ_P = SWA_QP + 2 * SWA_KVP
Q_BLOCK = 128
ALIBI_MAX = 8.0
MEM_HEADS = 4
MEM_HEAD_DIM = 128
MEM_DIM = MEM_HEADS * MEM_HEAD_DIM
N_EXPERTS = 32
TOP_K = 4
D_FF = 1024
SWIGLU_ALPHA = 1.702
SWIGLU_LIMIT = 7.0
EXPERT_BLOCK = 512
NORM_EPS = 1e-5
CHUNK = 64
NEG_BIG = -1e30

VMEM_LIMIT = 56 * 1024 * 1024


def _cparams(*sem):
    return pltpu.CompilerParams(dimension_semantics=sem, vmem_limit_bytes=VMEM_LIMIT)


def _rms(x, w):
    return x * lax.rsqrt(jnp.mean(x * x, axis=-1, keepdims=True) + NORM_EPS) * w


def _dot(a, b):
    return jnp.dot(a, b, preferred_element_type=F32)


def _dot_nt(a, b):
    return lax.dot_general(a, b, (((1,), (1,)), ((), ())), preferred_element_type=F32)


def _split2(x):
    hi = x.astype(BF16)
    lo = (x - hi.astype(F32)).astype(BF16)
    return hi, lo


def _split3(x):
    hi = x.astype(BF16)
    r1 = x - hi.astype(F32)
    mid = r1.astype(BF16)
    lo = (r1 - mid.astype(F32)).astype(BF16)
    return hi, mid, lo


U32 = jnp.uint32
ROW_WORDS = D_MODEL // 2
ROW_TILES = ROW_WORDS // LANES
HIGH_HALF = np.uint32(0xFFFF0000)


def _bf16_bits(x):
    return lax.bitcast_convert_type(x.astype(BF16).astype(F32), U32)


def _pack_rows(x):
    return (_bf16_bits(x[:, :ROW_WORDS]) >> 16) | (_bf16_bits(x[:, ROW_WORDS:]) & HIGH_HALF)


def _unpack_rows(w):
    lo = lax.bitcast_convert_type(w << 16, F32)
    hi = lax.bitcast_convert_type(w & HIGH_HALF, F32)
    return jnp.concatenate([lo, hi], axis=1)


def _store_row_tiles(ref, w):
    rows = w.shape[0]
    for c in range(ROW_TILES):
        ref[pl.ds(c, rows, stride=ROW_TILES), :] = w[:, c * LANES:(c + 1) * LANES]


def _load_row_tiles(ref, rows):
    return jnp.concatenate(
        [ref[pl.ds(c, rows, stride=ROW_TILES), :] for c in range(ROW_TILES)], axis=1)


def _group_ones(n, group):
    r = lax.broadcasted_iota(I32, (n, n), 0) // group
    c = lax.broadcasted_iota(I32, (n, n), 1) // group
    return (r == c).astype(BF16)


def _group_sum(x, ones):
    hi, lo = _split2(x)
    return _dot(hi, ones) + _dot(lo, ones)


def _inproj_kernel(x_ref, nw_ref, wr_ref, ws_ref, wq_ref, bs_ref, pr_ref, ps_ref, qm_ref):
    xb = _rms(x_ref[...], nw_ref[...]).astype(BF16)
    pr_ref[...] = _dot(xb, wr_ref[...])
    ps_ref[...] = (_dot(xb, ws_ref[...]) + bs_ref[...]).astype(BF16)
    qm_ref[...] = _dot(xb, wq_ref[...]).astype(BF16)


def _inproj(x2, nw, w_rw, w_sw, w_qm, b_sw, tm):
    n = x2.shape[0]
    const = lambda i: (0, 0)
    row = lambda i: (i, 0)
    return pl.pallas_call(
        _inproj_kernel,
        grid=(n // tm,),
        in_specs=[
            pl.BlockSpec((tm, D_MODEL), row),
            pl.BlockSpec((1, D_MODEL), const),
            pl.BlockSpec((D_MODEL, RWKV_PROJ), const),
            pl.BlockSpec((D_MODEL, SWA_PROJ_P), const),
            pl.BlockSpec((D_MODEL, MEM_DIM), const),
            pl.BlockSpec((1, SWA_PROJ_P), const),
        ],
        out_specs=[
            pl.BlockSpec((tm, RWKV_PROJ), row),
            pl.BlockSpec((tm, SWA_PROJ_P), row),
            pl.BlockSpec((tm, MEM_DIM), row),
        ],
        out_shape=[
            jax.ShapeDtypeStruct((n, RWKV_PROJ), F32),
            jax.ShapeDtypeStruct((n, SWA_PROJ_P), BF16),
            jax.ShapeDtypeStruct((n, MEM_DIM), BF16),
        ],
        compiler_params=_cparams("parallel"),
        name="inproj",
    )(x2, nw, w_rw, w_sw, w_qm, b_sw)


def _memkv_kernel(m_ref, nw_ref, w_ref, o_ref):
    mb = _rms(m_ref[...], nw_ref[...]).astype(BF16)
    o_ref[...] = _dot(mb, w_ref[...]).astype(BF16)


def _memkv(mem2, nw, w, tm):
    n = mem2.shape[0]
    return pl.pallas_call(
        _memkv_kernel,
        grid=(n // tm,),
        in_specs=[
            pl.BlockSpec((tm, D_MODEL), lambda i: (i, 0)),
            pl.BlockSpec((1, D_MODEL), lambda i: (0, 0)),
            pl.BlockSpec((D_MODEL, 2 * MEM_DIM), lambda i: (0, 0)),
        ],
        out_specs=pl.BlockSpec((tm, 2 * MEM_DIM), lambda i: (i, 0)),
        out_shape=jax.ShapeDtypeStruct((n, 2 * MEM_DIM), BF16),
        compiler_params=_cparams("parallel"),
        name="memkv",
    )(mem2, nw, w)


def _rwkv_prep_kernel(p_ref, halo_ref, mu_ref, w0_ref, wwa_ref, a0_ref, g2_ref, kk_ref, ka_ref,
                      rk_ref, r_out, lw_out, k_out, v_out, kk_out, b_out, g_out, bonus_out,
                      *, tiles_per_seq):
    i = pl.program_id(0)
    p = p_ref[...]
    tm = p.shape[0]
    halo = jnp.where(i % tiles_per_seq == 0, 0.0, halo_ref[7:8, :])
    row = lax.broadcasted_iota(I32, p.shape, 0)
    prev = jnp.where(row == 0, halo, pltpu.roll(p, 1, axis=0))
    ps = p + (prev - p) * mu_ref[...]
    r = ps[:, 0:RWKV_DIM]
    k = ps[:, RWKV_DIM:2 * RWKV_DIM]
    v = ps[:, 2 * RWKV_DIM:3 * RWKV_DIM]
    wa = ps[:, 3 * RWKV_DIM:3 * RWKV_DIM + LANES]
    pg = ps[:, 3 * RWKV_DIM + LANES:]
    lane = lax.broadcasted_iota(I32, (tm, LANES), 1)
    z = jnp.where(lane < DECAY_LORA, jnp.tanh(wa), wa).astype(BF16)
    lora = _dot(z, wwa_ref[...])
    w_log = -jax.nn.softplus(-(w0_ref[...] + lora[:, :RWKV_DIM])) - 0.5
    lw_out[...] = -jnp.exp(w_log)
    alr = jax.nn.sigmoid(a0_ref[...] + lora[:, RWKV_DIM:])
    g_out[...] = _dot(jax.nn.sigmoid(pg).astype(BF16), g2_ref[...]).astype(BF16)
    ones = _group_ones(RWKV_DIM, HEAD_DIM)
    kk = k * kk_ref[...]
    kk = kk / jnp.maximum(jnp.sqrt(_group_sum(kk * kk, ones)), 1e-12)
    k2 = k * (1.0 + (alr - 1.0) * ka_ref[...])
    r_out[...] = r.astype(BF16)
    k_out[...] = k2.astype(BF16)
    v_out[...] = v.astype(BF16)
    kk_out[...] = kk.astype(BF16)
    b_out[...] = (kk * alr).astype(BF16)
    bonus_out[...] = (_group_sum(r * k2 * rk_ref[...], ones) * v).astype(BF16)


def _rwkv_prep(p_rw, mu, w0, wwa, a0, g2, k_k, k_a, r_k, tm, seq):
    n = p_rw.shape[0]
    tiles_per_seq = seq // tm
    const = lambda i: (0, 0)
    row = lambda i: (i, 0)
    vec = pl.BlockSpec((1, RWKV_DIM), const)
    out = pl.BlockSpec((tm, RWKV_DIM), row)
    return pl.pallas_call(
        functools.partial(_rwkv_prep_kernel, tiles_per_seq=tiles_per_seq),
        grid=(n // tm,),
        in_specs=[
            pl.BlockSpec((tm, RWKV_PROJ), row),
            pl.BlockSpec((8, RWKV_PROJ), lambda i: (jnp.maximum(i * (tm // 8) - 1, 0), 0)),
            pl.BlockSpec((1, RWKV_PROJ), const),
            vec,
            pl.BlockSpec((LANES, 2 * RWKV_DIM), const),
            vec,
            pl.BlockSpec((GATE_LORA, RWKV_DIM), const),
            vec, vec, vec,
        ],
        out_specs=[out] * 8,
        out_shape=[jax.ShapeDtypeStruct((n, RWKV_DIM), F32 if i == 1 else BF16) for i in range(8)],
        compiler_params=_cparams("parallel"),
        name="rwkv_prep",
    )(p_rw, p_rw, mu, w0, wwa, a0, g2, k_k, k_a, r_k)


def _blockdiag(x, first_head):
    zero = jnp.zeros_like(x)
    return jnp.concatenate([jnp.where(first_head, x, zero), jnp.where(first_head, zero, x)], axis=0)


def _rwkv_scan_kernel(r_ref, lw_ref, k_ref, v_ref, kk_ref, b_ref, g_ref, bonus_ref, lnw_ref,
                      lnb_ref, y_ref, s_ref, *, chunks):
    @pl.when(pl.program_id(1) == 0)
    def _():
        s_ref[...] = jnp.zeros_like(s_ref)

    c2 = 2 * CHUNK
    tri_incl = (lax.broadcasted_iota(I32, (CHUNK, CHUNK), 1)
                <= lax.broadcasted_iota(I32, (CHUNK, CHUNK), 0)).astype(BF16)
    rowi = lax.broadcasted_iota(I32, (c2, c2), 0)
    coli = lax.broadcasted_iota(I32, (c2, c2), 1)
    strict = coli < rowi
    incl = coli <= rowi
    eye = (coli == rowi).astype(F32)
    first_head = lax.broadcasted_iota(I32, (CHUNK, LANES), 1) < HEAD_DIM
    ones = _group_ones(RWKV_DIM, HEAD_DIM)
    inv_n = 1.0 / HEAD_DIM

    seqs = r_ref.shape[0]
    pairs = range(RWKV_PAIRS)
    lanes = [slice(j * LANES, (j + 1) * LANES) for j in pairs]
    units = [(q, j) for q in range(seqs) for j in pairs]

    def chunk_body(c, carry):
        rows = pl.ds(pl.multiple_of(c * CHUNK, CHUNK), CHUNK)
        a_t, r_t, b_t, k_t, b_h, k_h, p_end, v = [], [], [], [], [], [], [], []
        for q in range(seqs):
            lw = lw_ref[q, rows, :]
            hi, mid, lo = _split3(lw)
            cl = _dot(tri_incl, hi) + _dot(tri_incl, mid) + _dot(tri_incl, lo)
            cl_end = cl[CHUNK - 1:CHUNK, :]
            kq = k_ref[q, rows, :]
            bq = b_ref[q, rows, :]
            inv = jnp.exp(-cl)
            to_end = jnp.exp(cl_end - cl)
            a_t.append(-kk_ref[q, rows, :] * jnp.exp(cl - lw))
            r_t.append(r_ref[q, rows, :] * jnp.exp(cl))
            b_t.append(bq * inv)
            k_t.append(kq * inv)
            b_h.append(bq * to_end)
            k_h.append(kq * to_end)
            p_end.append(jnp.exp(cl_end))
            v.append(v_ref[q, rows, :])
        bd = lambda x, u: _blockdiag(x[u[0]][:, lanes[u[1]]], first_head).astype(BF16)
        lhs = [jnp.concatenate([bd(a_t, u), bd(r_t, u)], axis=0) for u in units]
        a4 = [_dot_nt(lhs[i], jnp.concatenate([bd(b_t, u), bd(k_t, u)], axis=0))
              for i, u in enumerate(units)]
        ids = range(len(units))
        a_ab = [jnp.where(strict, a4[i][:c2, :c2], 0.0) for i in ids]
        tinv = [eye + a_ab[i] for i in ids]
        ab = [a_ab[i].astype(BF16) for i in ids]
        apow = [_dot(ab[i], ab[i]) for i in ids]
        for _ in range(4):
            st = [_dot(jnp.concatenate([tinv[i], apow[i]], axis=0).astype(BF16),
                       apow[i].astype(BF16)) for i in ids]
            tinv = [tinv[i] + st[i][:c2] for i in ids]
            apow = [st[i][c2:] for i in ids]
        tinv = [(tinv[i] + _dot(tinv[i].astype(BF16), apow[i].astype(BF16))).astype(BF16)
                for i in ids]
        v2 = [bd(v, u) for u in units]
        a_ak = [jnp.where(strict, a4[i][:c2, c2:], 0.0).astype(BF16) for i in ids]
        a_r = [jnp.concatenate([jnp.where(incl, a4[i][c2:, :c2], 0.0),
                                jnp.where(incl, a4[i][c2:, c2:], 0.0)], axis=1).astype(BF16)
               for i in ids]
        akv = [_dot(a_ak[i], v2[i]) for i in ids]
        s_old = [s_ref[i] for i in ids]
        su = [_dot_nt(lhs[i], s_old[i].astype(BF16)) for i in ids]
        sa = [_dot(tinv[i], (su[i][:c2] + akv[i]).astype(BF16)) for i in ids]
        y2 = [su[i][c2:] + _dot(a_r[i], jnp.concatenate([sa[i].astype(BF16), v2[i]], axis=0))
              for i in ids]
        sav_t = [jnp.concatenate([sa[i], v2[i].astype(F32)], axis=0).T.astype(BF16) for i in ids]
        for i, u in enumerate(units):
            s_ref[i] = s_old[i] * p_end[u[0]][:, lanes[u[1]]] + _dot(
                sav_t[i], jnp.concatenate([bd(b_h, u), bd(k_h, u)], axis=0))
        for q in range(seqs):
            y = jnp.concatenate([y2[q * RWKV_PAIRS + j][:CHUNK] + y2[q * RWKV_PAIRS + j][CHUNK:]
                                 for j in pairs], axis=1)
            m = _group_sum(y, ones) * inv_n
            d = y - m
            var = _group_sum(d * d, ones) * inv_n
            yn = d * lax.rsqrt(var + LNX_EPS) * lnw_ref[...] + lnb_ref[...]
            y_ref[q, rows, :] = ((yn + bonus_ref[q, rows, :]) * g_ref[q, rows, :]).astype(BF16)
        return carry

    lax.fori_loop(0, chunks, chunk_body, 0)


def _rwkv_scan(r, lw, k, v, kk, b, g, bonus, lnw, lnb, batch, seq, tc, seqs):
    as3 = lambda a: a.reshape(batch, seq, RWKV_DIM)
    blk = pl.BlockSpec((seqs, tc, RWKV_DIM), lambda bi, ti: (bi, ti, 0))
    vec = pl.BlockSpec((1, RWKV_DIM), lambda bi, ti: (0, 0))
    y = pl.pallas_call(
        functools.partial(_rwkv_scan_kernel, chunks=tc // CHUNK),
        grid=(batch // seqs, seq // tc),
        in_specs=[blk] * 8 + [vec, vec],
        out_specs=blk,
        out_shape=jax.ShapeDtypeStruct((batch, seq, RWKV_DIM), BF16),
        scratch_shapes=[pltpu.VMEM((seqs * RWKV_PAIRS, LANES, LANES), F32)],
        compiler_params=_cparams("parallel", "arbitrary"),
        name="rwkv_scan",
    )(*(as3(a) for a in (r, lw, k, v, kk, b, g, bonus)), lnw, lnb)
    return y.reshape(batch * seq, RWKV_DIM)


def _swa_kernel(sink_ref, q_ref, kvc_ref, kvp_ref, o_ref):
    nblk = pl.program_id(1)
    kv = jnp.concatenate([kvp_ref[...], kvc_ref[...]], axis=0)
    rows = SWA_GROUP * Q_BLOCK
    row = lax.broadcasted_iota(I32, (rows, 2 * Q_BLOCK), 0)
    gi_of_row = row // Q_BLOCK
    i = row % Q_BLOCK
    j = lax.broadcasted_iota(I32, (rows, 2 * Q_BLOCK), 1)
    dist = i + Q_BLOCK - j
    valid = (dist >= 0) & (dist < Q_BLOCK) & ((j >= Q_BLOCK) | (nblk > 0))
    distf = dist.astype(F32)
    gi_col = lax.broadcasted_iota(I32, (rows, 1), 0) // Q_BLOCK
    for h in range(SWA_KV_HEADS):
        kh = kv[:, h * LANES:(h + 1) * LANES]
        vh = kv[:, SWA_KVP + h * LANES:SWA_KVP + (h + 1) * LANES]
        heads = [h * SWA_GROUP + gi for gi in range(SWA_GROUP)]
        q4 = jnp.concatenate([q_ref[:, hq * LANES:(hq + 1) * LANES] for hq in heads], axis=0)
        bias = jnp.zeros((rows, 2 * Q_BLOCK), F32)
        sink = jnp.zeros((rows, 1), F32)
        for gi, hq in enumerate(heads):
            slope = 2.0 ** (-ALIBI_MAX * (hq + 1) / SWA_Q_HEADS)
            bias = jnp.where(gi_of_row == gi, slope * distf, bias)
            sink = jnp.where(gi_col == gi, sink_ref[hq], sink)
        s = jnp.where(valid, _dot_nt(q4, kh) - bias, NEG_BIG)
        m = jnp.maximum(jnp.max(s, axis=-1, keepdims=True), sink)
        e = jnp.exp(s - m)
        den = jnp.sum(e, axis=-1, keepdims=True) + jnp.exp(sink - m)
        o4 = _dot((e / den).astype(BF16), vh).astype(BF16)
        for gi, hq in enumerate(heads):
            o_ref[:, hq * LANES:(hq + 1) * LANES] = o4[gi * Q_BLOCK:(gi + 1) * Q_BLOCK]


def _swa(p_sw, sinks, batch, seq):
    n = p_sw.shape[0]
    nb = seq // Q_BLOCK
    kv_col = SWA_QP // (2 * SWA_KVP)
    return pl.pallas_call(
        _swa_kernel,
        grid=(batch, nb),
        in_specs=[
            pl.BlockSpec(memory_space=pltpu.SMEM),
            pl.BlockSpec((Q_BLOCK, SWA_QP), lambda bi, ni: (bi * nb + ni, 0)),
            pl.BlockSpec((Q_BLOCK, 2 * SWA_KVP), lambda bi, ni: (bi * nb + ni, kv_col)),
            pl.BlockSpec((Q_BLOCK, 2 * SWA_KVP),
                         lambda bi, ni: (bi * nb + jnp.maximum(ni - 1, 0), kv_col)),
        ],
        out_specs=pl.BlockSpec((Q_BLOCK, SWA_QP), lambda bi, ni: (bi * nb + ni, 0)),
        out_shape=jax.ShapeDtypeStruct((n, SWA_QP), BF16),
        compiler_params=_cparams("parallel", "parallel"),
        name="swa",
    )(sinks, p_sw, p_sw, p_sw)


def _merge_kernel(x_ref, yr_ref, ys_ref, qm_ref, km_ref, vm_ref, anw_ref, wg_ref, wb0_ref, wb1_ref,
                  wb2_ref, wo_ref, fnw_ref, wrh_ref, wrl_ref, br_ref,
                  h_ref, hn_ref, meta_ref, cnt_ref, carry_ref):
    @pl.when(pl.program_id(0) == 0)
    def _():
        carry_ref[...] = jnp.zeros_like(carry_ref)

    x = x_ref[...]
    tm = x.shape[0]
    xb = _rms(x, anw_ref[...]).astype(BF16)
    outs = []
    for hh in range(MEM_HEADS):
        cols = slice(hh * MEM_HEAD_DIM, (hh + 1) * MEM_HEAD_DIM)
        s = _dot_nt(qm_ref[:, cols], km_ref[:, cols]) * (MEM_HEAD_DIM ** -0.5)
        m = jnp.max(s, axis=-1, keepdims=True)
        e = jnp.exp(s - m)
        pr = (e / jnp.sum(e, axis=-1, keepdims=True)).astype(BF16)
        outs.append(_dot(pr, vm_ref[:, cols]))
    y_mem = jnp.concatenate(outs, axis=1).astype(BF16)
    merged = jax.nn.sigmoid(_dot(xb, wg_ref[0])) * _dot(yr_ref[...], wb0_ref[...])
    merged += jax.nn.sigmoid(_dot(xb, wg_ref[1])) * _dot(ys_ref[...], wb1_ref[...])
    merged += jax.nn.sigmoid(_dot(xb, wg_ref[2])) * _dot(y_mem, wb2_ref[...])
    h = x + _dot(merged.astype(BF16), wo_ref[...])
    h_ref[...] = h
    hn = _rms(h, fnw_ref[...])
    _store_row_tiles(hn_ref, _pack_rows(hn))
    hi, lo = _split2(hn)
    logits = _dot(hi, wrh_ref[...]) + _dot(hi, wrl_ref[...]) + _dot(lo, wrh_ref[...]) + br_ref[...]
    lane = lax.broadcasted_iota(I32, (tm, LANES), 1).astype(F32)
    work = logits
    vals, idxs, sels = [], [], []
    for _ in range(TOP_K):
        mk = jnp.max(work, axis=-1, keepdims=True)
        ik = jnp.min(jnp.where(work == mk, lane, float(LANES)), axis=-1, keepdims=True)
        sel = lane == ik
        work = jnp.where(sel, 2.0 * NEG_BIG, work)
        vals.append(mk)
        idxs.append(ik)
        sels.append(sel)
    es = [jnp.exp(vk - vals[0]) for vk in vals]
    den = es[0] + es[1] + es[2] + es[3]
    cnt = jnp.zeros((tm, LANES), F32)
    for sel in sels:
        cnt = cnt + sel.astype(F32)
    tri = (lax.broadcasted_iota(I32, (tm, tm), 1) < lax.broadcasted_iota(I32, (tm, tm), 0)).astype(BF16)
    prefix = _dot(tri, cnt.astype(BF16)) + carry_ref[0:1, :]
    meta = jnp.zeros((tm, LANES), F32)
    for kk in range(TOP_K):
        pos = jnp.sum(jnp.where(sels[kk], prefix, 0.0), axis=-1, keepdims=True)
        meta = jnp.where(lane == kk, idxs[kk], meta)
        meta = jnp.where(lane == TOP_K + kk, es[kk] / den, meta)
        meta = jnp.where(lane == 2 * TOP_K + kk, pos, meta)
    meta_ref[...] = meta
    total = carry_ref[0:1, :] + jnp.sum(cnt, axis=0, keepdims=True)
    carry_ref[...] = jnp.broadcast_to(total, carry_ref.shape)
    cnt_ref[...] = jnp.broadcast_to(total, cnt_ref.shape)


def _merge(x2, y_rw, y_sw, q_mem, memkv, anw, wg, wb0, wb1, wb2, wo, fnw, wrh, wrl, br, tm, seq,
           mem_len):
    n = x2.shape[0]
    tiles_per_seq = seq // tm
    row = lambda i: (i, 0)
    const = lambda i: (0, 0)
    return pl.pallas_call(
        _merge_kernel,
        grid=(n // tm,),
        in_specs=[
            pl.BlockSpec((tm, D_MODEL), row),
            pl.BlockSpec((tm, RWKV_DIM), row),
            pl.BlockSpec((tm, SWA_QP), row),
            pl.BlockSpec((tm, MEM_DIM), row),
            pl.BlockSpec((mem_len, MEM_DIM), lambda i: (i // tiles_per_seq, 0)),
            pl.BlockSpec((mem_len, MEM_DIM), lambda i: (i // tiles_per_seq, 1)),
            pl.BlockSpec((1, D_MODEL), const),
            pl.BlockSpec((3, D_MODEL, D_MODEL), lambda i: (0, 0, 0), pipeline_mode=pl.Buffered(1)),
            pl.BlockSpec((RWKV_DIM, D_MODEL), const, pipeline_mode=pl.Buffered(1)),
            pl.BlockSpec((SWA_QP, D_MODEL), const, pipeline_mode=pl.Buffered(1)),
            pl.BlockSpec((MEM_DIM, D_MODEL), const, pipeline_mode=pl.Buffered(1)),
            pl.BlockSpec((D_MODEL, D_MODEL), const, pipeline_mode=pl.Buffered(1)),
            pl.BlockSpec((1, D_MODEL), const),
            pl.BlockSpec((D_MODEL, LANES), const),
            pl.BlockSpec((D_MODEL, LANES), const),
            pl.BlockSpec((1, LANES), const),
        ],
        out_specs=[
            pl.BlockSpec((tm, D_MODEL), row),
            pl.BlockSpec((tm * ROW_TILES, LANES), row),
            pl.BlockSpec((tm, LANES), row),
            pl.BlockSpec((8, LANES), const),
        ],
        out_shape=[
            jax.ShapeDtypeStruct((n, D_MODEL), F32),
            jax.ShapeDtypeStruct((n * ROW_TILES, LANES), U32),
            jax.ShapeDtypeStruct((n, LANES), F32),
            jax.ShapeDtypeStruct((8, LANES), F32),
        ],
        scratch_shapes=[pltpu.VMEM((8, LANES), F32)],
        compiler_params=_cparams("arbitrary"),
        name="merge_router",
    )(x2, y_rw, y_sw, q_mem, memkv, memkv, anw, wg, wb0, wb1, wb2, wo, fnw, wrh, wrl, br)


def _stage_expert_weights(w1_ref, w2_ref, w1g_s, w1l_s, w2_s, t_s):
    for r in range(D_MODEL // LANES):
        rows = slice(r * LANES, (r + 1) * LANES)
        t_s[...] = w1_ref[rows, :].T
        w1g_s[rows, :] = t_s[pl.ds(0, D_FF, stride=2), :].T.astype(BF16)
        w1l_s[rows, :] = t_s[pl.ds(1, D_FF, stride=2), :].T.astype(BF16)
    w2_s[...] = w2_ref[...].astype(BF16)


IDX_RING = 4
ROW_BUFS = 2
MOE_EXTRA_BLOCKS = 2


def _moe_kernel(be_ref, idx_hbm, hn_hbm, w1_ref, b1g_ref, b1l_ref, w2_ref, b2_ref, y_hbm,
                idx_smem, xbuf, ybuf, w1g_s, w1l_s, w2_s, t_s, gsem, ssem, isem):
    i = pl.program_id(0)
    last = pl.num_programs(0) - 1
    n_dump = y_hbm.shape[0] - 2 * EXPERT_BLOCK * ROW_TILES

    def tile_rows(start):
        return pl.ds(pl.multiple_of(start, ROW_TILES), ROW_TILES)

    def idx_copy(blk):
        ring = blk % IDX_RING
        return pltpu.make_async_copy(idx_hbm.at[blk], idx_smem.at[ring], isem.at[ring])

    def gather(blk, j):
        slot = blk % ROW_BUFS
        src = idx_smem[blk % IDX_RING, j]
        return pltpu.make_async_copy(hn_hbm.at[tile_rows(src), :],
                                     xbuf.at[slot, pl.ds(j * ROW_TILES, ROW_TILES), :], gsem.at[slot])

    def scatter(blk, j, dst=None):
        slot = blk % ROW_BUFS
        if dst is None:
            dst = idx_smem[blk % IDX_RING, EXPERT_BLOCK + j]
        return pltpu.make_async_copy(ybuf.at[slot, pl.ds(j * ROW_TILES, ROW_TILES), :],
                                     y_hbm.at[tile_rows(dst), :], ssem.at[slot])

    @pl.when(i == 0)
    def _():
        ybuf[...] = jnp.zeros_like(ybuf)
        for slot in range(ROW_BUFS):
            for j in range(EXPERT_BLOCK):
                scatter(slot, j, dst=n_dump + (slot * EXPERT_BLOCK + j) * ROW_TILES).start()
        idx_copy(0).start()
        idx_copy(0).wait()
        for j in range(EXPERT_BLOCK):
            gather(0, j).start()
        idx_copy(1).start()

    idx_copy(i + 1).wait()
    for j in range(EXPERT_BLOCK):
        gather(i + 1, j).start(priority=j % 2)
    for j in range(EXPERT_BLOCK):
        scatter(i, j, dst=0).wait()
    idx_copy(i + 2).start()

    @pl.when((i == 0) | (be_ref[i] != be_ref[jnp.maximum(i - 1, 0)]))
    def _():
        _stage_expert_weights(w1_ref, w2_ref, w1g_s, w1l_s, w2_s, t_s)

    for j in range(EXPERT_BLOCK):
        gather(i, j).wait()

    slot = i % ROW_BUFS
    xb = _unpack_rows(_load_row_tiles(xbuf.at[slot], EXPERT_BLOCK)).astype(BF16)
    glu = _dot(xb, w1g_s[...]) + b1g_ref[...]
    lin = _dot(xb, w1l_s[...]) + b1l_ref[...]
    glu = jnp.minimum(glu, SWIGLU_LIMIT)
    lin = jnp.clip(lin, -SWIGLU_LIMIT, SWIGLU_LIMIT)
    act = glu * jax.nn.sigmoid(SWIGLU_ALPHA * glu) * (lin + 1.0)
    yv = _dot(act.astype(BF16), w2_s[...]) + b2_ref[...]
    _store_row_tiles(ybuf.at[slot], _pack_rows(yv))
    for j in range(EXPERT_BLOCK):
        scatter(i, j).start(priority=j % 2)

    @pl.when(i == last)
    def _():
        for blk in (i - 1, i):
            for j in range(EXPERT_BLOCK):
                scatter(blk, j, dst=0).wait()
        for j in range(EXPERT_BLOCK):
            gather(i + 1, j).wait()
        idx_copy(i + 2).wait()


def _moe_ffn(block_e, slot_idx, hn, w1, b1g, b1l, w2, b2, n_rows):
    n_blocks = slot_idx.shape[0] - MOE_EXTRA_BLOCKS
    bspec = pl.BlockSpec((None, 1, D_FF), lambda i, be: (be[i], 0, 0))
    grid_spec = pltpu.PrefetchScalarGridSpec(
        num_scalar_prefetch=1,
        grid=(n_blocks,),
        in_specs=[
            pl.BlockSpec(memory_space=pl.ANY),
            pl.BlockSpec(memory_space=pl.ANY),
            pl.BlockSpec((None, D_MODEL, 2 * D_FF), lambda i, be: (be[i], 0, 0)),
            bspec, bspec,
            pl.BlockSpec((None, D_FF, D_MODEL), lambda i, be: (be[i], 0, 0)),
            pl.BlockSpec((None, 1, D_MODEL), lambda i, be: (be[i], 0, 0)),
        ],
        out_specs=pl.BlockSpec(memory_space=pl.ANY),
        scratch_shapes=[
            pltpu.SMEM((IDX_RING, 2 * EXPERT_BLOCK), I32),
            pltpu.VMEM((ROW_BUFS, EXPERT_BLOCK * ROW_TILES, LANES), U32),
            pltpu.VMEM((ROW_BUFS, EXPERT_BLOCK * ROW_TILES, LANES), U32),
            pltpu.VMEM((D_MODEL, D_FF), BF16),
            pltpu.VMEM((D_MODEL, D_FF), BF16),
            pltpu.VMEM((D_FF, D_MODEL), BF16),
            pltpu.VMEM((2 * D_FF, LANES), F32),
            pltpu.SemaphoreType.DMA((ROW_BUFS,)),
            pltpu.SemaphoreType.DMA((ROW_BUFS,)),
            pltpu.SemaphoreType.DMA((IDX_RING,)),
        ],
    )
    return pl.pallas_call(
        _moe_kernel,
        grid_spec=grid_spec,
        out_shape=jax.ShapeDtypeStruct(((n_rows + 2 * EXPERT_BLOCK) * ROW_TILES, LANES), U32),
        compiler_params=_cparams("arbitrary"),
        name="moe_ffn",
    )(block_e, slot_idx, hn, w1, b1g, b1l, w2, b2)


def _combine_kernel(h_ref, y0_ref, y1_ref, y2_ref, y3_ref, meta_ref, fw_ref, o_ref):
    acc = h_ref[...]
    meta = meta_ref[...]
    for kk, y_ref in enumerate((y0_ref, y1_ref, y2_ref, y3_ref)):
        y = _unpack_rows(_load_row_tiles(y_ref, acc.shape[0]))
        acc = acc + meta[:, TOP_K + kk:TOP_K + kk + 1] * y
    o_ref[...] = _rms(acc, fw_ref[...])


def _combine(h, y_flat, meta, fw, tm):
    n = h.shape[0]
    tiles = n // tm
    row = lambda i: (i, 0)
    y_specs = [pl.BlockSpec((tm * ROW_TILES, LANES),
                            functools.partial(lambda i, kk: (kk * tiles + i, 0), kk=kk))
               for kk in range(TOP_K)]
    return pl.pallas_call(
        _combine_kernel,
        grid=(tiles,),
        in_specs=[pl.BlockSpec((tm, D_MODEL), row)] + y_specs + [
            pl.BlockSpec((tm, LANES), row),
            pl.BlockSpec((1, D_MODEL), lambda i: (0, 0)),
        ],
        out_specs=pl.BlockSpec((tm, D_MODEL), row),
        out_shape=jax.ShapeDtypeStruct((n, D_MODEL), F32),
        compiler_params=_cparams("parallel"),
        name="combine",
    )(h, y_flat, y_flat, y_flat, y_flat, meta, fw)


def _pad_heads_cols(w, heads):
    lead = w.shape[:-1]
    w = w.reshape(lead + (heads, HEAD_DIM))
    w = jnp.pad(w, [(0, 0)] * len(lead) + [(0, 0), (0, LANES - HEAD_DIM)])
    return w.reshape(lead + (heads * LANES,))


def _pick_tile(n, pref):
    t = pref
    while n % t:
        t //= 2
    return t


def kernel(x, mem, attn_norm_w, mem_norm_w, w_in, b_swa, rwkv_mu, rwkv_w0, rwkv_w2, rwkv_a0,
           rwkv_a2, rwkv_g2, rwkv_kk, rwkv_ka, rwkv_rk, rwkv_lnx_w, rwkv_lnx_b, swa_sinks,
           w_mem_kv, w_gate, w_branch, w_out, ffn_norm_w, w_router, b_router, w_exp1, b_exp1,
           w_exp2, b_exp2, final_norm_w):
    batch, seq, d = x.shape
    mem_len = mem.shape[1]
    n = batch * seq
    layer = 0
    x2 = x.reshape(n, d)
    row = lambda a: a.reshape(1, -1)

    w_in_l = w_in[layer]
    w_rw = w_in_l[:, :RWKV_PROJ].astype(BF16)
    sw = w_in_l[:, RWKV_PROJ:RWKV_PROJ + SWA_DIM + 2 * SWA_KV_DIM]
    bs = b_swa[layer]
    scale = HEAD_DIM ** -0.5
    w_sw = jnp.concatenate([
        _pad_heads_cols(sw[:, :SWA_DIM] * scale, SWA_Q_HEADS),
        _pad_heads_cols(sw[:, SWA_DIM:SWA_DIM + SWA_KV_DIM], SWA_KV_HEADS),
        _pad_heads_cols(sw[:, SWA_DIM + SWA_KV_DIM:], SWA_KV_HEADS)], axis=1).astype(BF16)
    b_sw = row(jnp.concatenate([
        _pad_heads_cols(bs[:SWA_DIM] * scale, SWA_Q_HEADS),
        _pad_heads_cols(bs[SWA_DIM:SWA_DIM + SWA_KV_DIM], SWA_KV_HEADS),
        _pad_heads_cols(bs[SWA_DIM + SWA_KV_DIM:], SWA_KV_HEADS)]))
    w_qm = w_in_l[:, RWKV_PROJ + SWA_DIM + 2 * SWA_KV_DIM:].astype(BF16)
    zeros_l = jnp.zeros((DECAY_LORA, RWKV_DIM), F32)
    wwa = jnp.concatenate([
        jnp.concatenate([rwkv_w2[layer], zeros_l], axis=1),
        jnp.concatenate([zeros_l, rwkv_a2[layer]], axis=1)], axis=0).astype(BF16)
    wb = w_branch[layer]
    wb1 = _pad_heads_cols(wb[1].T, SWA_Q_HEADS).T.astype(BF16)
    wr = jnp.pad(w_router[layer], ((0, 0), (0, LANES - N_EXPERTS)))
    wrh = wr.astype(BF16)
    wrl = (wr - wrh.astype(F32)).astype(BF16)
    br = row(jnp.pad(b_router[layer], (0, LANES - N_EXPERTS), constant_values=NEG_BIG))
    w1 = w_exp1[layer]
    b1 = b_exp1[layer]
    b1g = b1[:, None, 0::2]
    b1l = b1[:, None, 1::2]
    w2 = w_exp2[layer]
    b2 = b_exp2[layer][:, None, :]

    tm_proj = _pick_tile(seq, 512)
    p_rw, p_sw, q_mem = _inproj(x2, row(attn_norm_w[layer]), w_rw, w_sw, w_qm, b_sw, tm_proj)
    memkv = _memkv(mem.reshape(batch * mem_len, d), row(mem_norm_w[layer]),
                   w_mem_kv[layer].astype(BF16), mem_len)
    r, lw, k, v, kk, b, g, bonus = _rwkv_prep(
        p_rw, row(rwkv_mu[layer]), row(rwkv_w0[layer]), wwa, row(rwkv_a0[layer]),
        rwkv_g2[layer].astype(BF16), row(rwkv_kk[layer]), row(rwkv_ka[layer]),
        row(rwkv_rk[layer]), tm_proj, seq)
    y_rw = _rwkv_scan(r, lw, k, v, kk, b, g, bonus, row(rwkv_lnx_w[layer]),
                      row(rwkv_lnx_b[layer]), batch, seq, _pick_tile(seq, 256),
                      4 if batch % 4 == 0 else 1)
    y_sw = _swa(p_sw, swa_sinks[layer], batch, seq)

    tm_merge = _pick_tile(seq, 512)
    h, hn, meta, counts = _merge(
        x2, y_rw, y_sw, q_mem, memkv, row(attn_norm_w[layer]), w_gate[layer].astype(BF16),
        wb[0].astype(BF16), wb1, wb[2].astype(BF16), w_out[layer].astype(BF16),
        row(ffn_norm_w[layer]), wrh, wrl, br, tm_merge, seq, mem_len)

    a_total = n * TOP_K
    e_idx = meta[:, 0:TOP_K].astype(I32)
    pos = meta[:, 2 * TOP_K:3 * TOP_K].astype(I32)
    cnt = counts[0, :N_EXPERTS].astype(I32)
    padded = ((cnt + EXPERT_BLOCK - 1) // EXPERT_BLOCK) * EXPERT_BLOCK
    pad_ends = jnp.cumsum(padded)
    pad_starts = pad_ends - padded
    n_blocks = -(-a_total // EXPERT_BLOCK) + N_EXPERTS
    rows_idx = n_blocks + MOE_EXTRA_BLOCKS
    p_slots = rows_idx * EXPERT_BLOCK
    dest = (pad_starts[e_idx] + pos).reshape(-1)
    a_ids = jnp.arange(a_total, dtype=I32)
    slot_a = jnp.full((p_slots,), -1, I32).at[dest].set(a_ids)
    slot_id = jnp.arange(p_slots, dtype=I32)
    slot_tok = jnp.where(slot_a >= 0, slot_a // TOP_K, 0)
    dump = a_total + slot_id % (ROW_BUFS * EXPERT_BLOCK)
    slot_dst = jnp.where(slot_a >= 0, (slot_a % TOP_K) * n + slot_a // TOP_K, dump)
    slot_idx = jnp.concatenate([slot_tok.reshape(rows_idx, EXPERT_BLOCK),
                                slot_dst.reshape(rows_idx, EXPERT_BLOCK)], axis=1) * ROW_TILES
    block_start = jnp.arange(n_blocks, dtype=I32) * EXPERT_BLOCK
    block_e = jnp.minimum(jnp.sum(pad_ends[None, :] <= block_start[:, None], axis=1),
                          N_EXPERTS - 1).astype(I32)

    y_flat = _moe_ffn(block_e, slot_idx, hn, w1, b1g, b1l, w2, b2, a_total)
    out = _combine(h, y_flat, meta, row(final_norm_w), tm_merge)
    return out.reshape(batch, seq, d)
```
